```python
import jax, jax.numpy as jnp
from jax import lax
import numpy as np

D_MODEL = 2048
BATCH = 8
SEQ = 4096
DEPTH = 4

CTX_LEN = 256
GRID_W = 64
MIX_W = 512
N_BRANCH = 4
GROUP_DIM = 128
N_GROUPS = MIX_W // GROUP_DIM
RET_CHUNK = 128
SGU_CHUNK = 128
CONV_W = 3
D_FF = -(-8 * D_MODEL // (3 * 256)) * 256
ROPE_BASE = 10000.0
RET_DECAY_EXP0 = 5
EPS = 1e-6

Q_OFF = 0
K_OFF = MIX_W
V_OFF = 2 * MIX_W
G_OFF = 3 * MIX_W
F_OFF = 4 * MIX_W
U_OFF = 5 * MIX_W
SC_OFF = 7 * MIX_W
GATE_OFF = 10 * MIX_W
IN_W = GATE_OFF + N_BRANCH * D_MODEL

kernel_name = "hybrid_retention_fourier_sgu_conv_dit"


def rmsnorm(x, g):
    xf = x.astype(jnp.float32)
    y = xf * lax.rsqrt(jnp.mean(xf * xf, axis=-1, keepdims=True) + EPS)
    return (y * g.astype(jnp.float32)).astype(x.dtype)


def group_rms(z):
    zf = z.astype(jnp.float32)
    return zf * lax.rsqrt(jnp.mean(zf * zf, axis=-1, keepdims=True) + EPS)


def modulate(h, shift, scale):
    return h * (1 + scale) + shift


def adaln(cond, w, b):
    m = jax.nn.silu(cond) @ w + b
    return jnp.split(m, 6, axis=-1)


def _rotate(x, pos):
    n = x.shape[-1]
    freqs = ROPE_BASE ** (-jnp.arange(0, n, 2, dtype=jnp.float32) / n)
    ang = pos.astype(jnp.float32)[:, None] * freqs[None, :]
    cos = jnp.cos(ang)[:, None, :].astype(x.dtype)
    sin = jnp.sin(ang)[:, None, :].astype(x.dtype)
    x1, x2 = x[..., : n // 2], x[..., n // 2:]
    return jnp.concatenate([x1 * cos - x2 * sin, x1 * sin + x2 * cos], axis=-1)


def rope_2d(x):
    L = x.shape[1]
    t = jnp.arange(L)
    h = x.shape[-1] // 2
    return jnp.concatenate([_rotate(x[..., :h], t // GRID_W), _rotate(x[..., h:], t % GRID_W)], axis=-1)


def retention_scan(q, k, v, log_g, s0):
    Bsz, L, H, d = q.shape
    C = RET_CHUNK
    n = L // C
    lg = log_g.astype(jnp.float32)
    qc = q.reshape(Bsz, n, C, H, d)
    kc = k.reshape(Bsz, n, C, H, d)
    vc = v.reshape(Bsz, n, C, H, d)
    i = jnp.arange(C, dtype=jnp.float32)
    rel = i[:, None] - i[None, :]
    decay = jnp.where(rel >= 0, jnp.exp(lg[:, None, None] * jnp.maximum(rel, 0.0)[None]), 0.0)
    scores = jnp.einsum('bnihd,bnjhd->bnhij', qc, kc) * decay[None, None]
    inner = jnp.einsum('bnhij,bnjhd->bnihd', scores, vc)
    k_w = jnp.exp(lg[None, :] * (C - 1 - i)[:, None])
    ds = jnp.einsum('bnjhd,bnjhe->nbhde', kc * k_w[None, None, :, :, None], vc)
    chunk_decay = jnp.exp(lg * C)[None, :, None, None]

    def step(s, ds_n):
        return chunk_decay * s + ds_n, s

    _, s_prev = lax.scan(step, s0.astype(jnp.float32), ds)
    q_w = jnp.exp(lg[None, :] * (i + 1)[:, None])
    cross = jnp.einsum('bnihd,nbhde->bnihe', qc * q_w[None, None, :, :, None], s_prev)
    return (inner + cross).reshape(Bsz, L, H, d)


def retention_state(k, v, log_g):
    L = k.shape[1]
    lg = log_g.astype(jnp.float32)
    w = jnp.exp(lg[None, :] * (L - 1 - jnp.arange(L, dtype=jnp.float32))[:, None])
    return jnp.einsum('blhd,blhe->bhde', k * w[None, :, :, None], v)


def bidir_retention(q, k, v, log_g2, s0_f, s0_b):
    fwd = retention_scan(q, k, v, log_g2[0], s0_f)
    bwd = retention_scan(jnp.flip(q, 1), jnp.flip(k, 1), jnp.flip(v, 1), log_g2[1], s0_b)
    return fwd + jnp.flip(bwd, 1)


def ctx_states(kv, log_g2):
    Bsz, L, _ = kv.shape
    k = kv[..., :MIX_W].astype(jnp.float32).reshape(Bsz, L, N_GROUPS, GROUP_DIM) * GROUP_DIM ** -0.5
    v = kv[..., MIX_W:].astype(jnp.float32).reshape(Bsz, L, N_GROUPS, GROUP_DIM)
    s_f = retention_state(k, v, log_g2[0])
    s_b = retention_state(jnp.flip(k, 1), jnp.flip(v, 1), log_g2[1])
    return s_f, s_b


def fourier_mix(z):
    Bsz, L, _ = z.shape
    zg = z.astype(jnp.float32).reshape(Bsz, L, N_GROUPS, GROUP_DIM)
    y = jnp.fft.fftn(zg, axes=(1, 3), norm="ortho").real
    return y.reshape(Bsz, L, MIX_W).astype(z.dtype)


def spatial_gating(z, w_s, b_s):
    Bsz, L, _ = z.shape
    z = jax.nn.gelu(z)
    u, v = z[..., :MIX_W], z[..., MIX_W:]
    vg = v.astype(jnp.float32).reshape(Bsz, L // SGU_CHUNK, SGU_CHUNK, N_GROUPS, GROUP_DIM)
    mu = jnp.mean(vg, axis=-1, keepdims=True)
    var = jnp.mean(jnp.square(vg - mu), axis=-1, keepdims=True)
    vg = (vg - mu) * lax.rsqrt(var + EPS)
    s = jnp.einsum('bnpgc,gqp->bnqgc', vg, w_s.astype(jnp.float32)) \
        + jnp.transpose(b_s.astype(jnp.float32))[None, None, :, :, None]
    return (u.astype(jnp.float32) * s.reshape(Bsz, L, MIX_W)).astype(z.dtype)


def conv3(y, w):
    C = y.shape[-1]
    return lax.conv_general_dilated(y, w[:, None, :].astype(y.dtype), window_strides=(1,),
                                    padding=((CONV_W // 2, CONV_W // 2),),
                                    dimension_numbers=('NWC', 'WIO', 'NWC'), feature_group_count=C)


def short_conv_mix(z, w, is_latent):
    Bsz, L, _ = z.shape
    b, cg, xv = z[..., :MIX_W], z[..., MIX_W:2 * MIX_W], z[..., 2 * MIX_W:]
    y = cg * xv
    if is_latent:
        rows = L // GRID_W
        y = conv3(y.reshape(Bsz * rows, GRID_W, MIX_W), w).reshape(Bsz, L, MIX_W)
    else:
        y = conv3(y, w)
    return b * y


def token_mix(proj, log_g2, conv_w, sgu_w, sgu_b, w_branch, w_out, s0_f, s0_b, is_latent):
    Bsz, L, _ = proj.shape
    heads = lambda z: z.astype(jnp.float32).reshape(Bsz, L, N_GROUPS, GROUP_DIM)
    q = heads(proj[..., Q_OFF:Q_OFF + MIX_W])
    k = heads(proj[..., K_OFF:K_OFF + MIX_W]) * GROUP_DIM ** -0.5
    v = heads(proj[..., V_OFF:V_OFF + MIX_W])
    g = proj[..., G_OFF:G_OFF + MIX_W]
    if is_latent:
        q, k = rope_2d(q), rope_2d(k)
    ret = bidir_retention(q, k, v, log_g2, s0_f, s0_b)
    y_a = jax.nn.silu(g) * group_rms(ret).reshape(Bsz, L, MIX_W).astype(g.dtype)
    y_b = fourier_mix(proj[..., F_OFF:F_OFF + MIX_W])
    y_c = spatial_gating(proj[..., U_OFF:U_OFF + 2 * MIX_W], sgu_w, sgu_b)
    y_d = short_conv_mix(proj[..., SC_OFF:SC_OFF + 3 * MIX_W], conv_w, is_latent)
    branches = jnp.stack([y_a, y_b, y_c, y_d], axis=2).astype(proj.dtype)
    gates = jax.nn.sigmoid(proj[..., GATE_OFF:].reshape(Bsz, L, N_BRANCH, D_MODEL))
    merged = jnp.sum(gates * jnp.einsum('blnm,nmd->blnd', branches, w_branch), axis=2)
    return merged @ w_out


def swiglu(h, w1, w2):
    a = h @ w1
    return (jax.nn.silu(a[..., :D_FF]) * a[..., D_FF:]) @ w2


def setup_inputs(seed: int = 0) -> dict:
    key = jax.random.key(seed)
    ks = jax.random.split(key, 16)
    f32 = jnp.float32
    nrm = lambda k, s: jax.random.normal(k, s, f32)
    base_lg = jnp.log1p(-jnp.exp2(-(RET_DECAY_EXP0 + jnp.arange(N_GROUPS, dtype=f32))))
    return {
        "x": nrm(ks[0], (BATCH, SEQ, D_MODEL)),
        "c": nrm(ks[1], (BATCH, D_MODEL)),
        "ctx": nrm(ks[2], (BATCH, CTX_LEN, D_MODEL)),
        "c_ctx": nrm(ks[3], (D_MODEL,)),
        "ada_w": nrm(ks[4], (DEPTH, D_MODEL, 6 * D_MODEL)) * (0.5 * D_MODEL ** -0.5),
        "ada_b": nrm(ks[5], (DEPTH, 6 * D_MODEL)) * 0.01,
        "norm_g": 1.0 + 0.02 * nrm(ks[6], (DEPTH, 4, D_MODEL)),
        "w_in": nrm(ks[7], (DEPTH, D_MODEL, IN_W)) * D_MODEL ** -0.5,
        "ret_log_decay": base_lg[None, None, :] * (1.0 + 0.05 * nrm(ks[8], (DEPTH, 2, N_GROUPS))),
        "conv_w": nrm(ks[9], (DEPTH, CONV_W, MIX_W)) * CONV_W ** -0.5,
        "sgu_w": nrm(ks[10], (DEPTH, N_GROUPS, SGU_CHUNK, SGU_CHUNK)) * SGU_CHUNK ** -0.5,
        "sgu_b": 0.02 * nrm(ks[11], (DEPTH, N_GROUPS, SGU_CHUNK)),
        "w_branch": nrm(ks[12], (DEPTH, N_BRANCH, MIX_W, D_MODEL)) * MIX_W ** -0.5,
        "w_out": nrm(ks[13], (DEPTH, D_MODEL, D_MODEL)) * D_MODEL ** -0.5,
        "ffn_w_in": nrm(ks[14], (DEPTH, D_MODEL, 2 * D_FF)) * D_MODEL ** -0.5,
        "ffn_w_out": nrm(ks[15], (DEPTH, D_FF, D_MODEL)) * D_FF ** -0.5,
    }


def reference(x, c, ctx, c_ctx, ada_w, ada_b, norm_g, w_in, ret_log_decay, conv_w, sgu_w, sgu_b,
              w_branch, w_out, ffn_w_in, ffn_w_out):
    for l in range(DEPTH):
        last = l == DEPTH - 1
        sx1, cx1, gx1, sx2, cx2, gx2 = [m[:, None, :] for m in adaln(c, ada_w[l], ada_b[l])]
        sc1, cc1, gc1, sc2, cc2, gc2 = adaln(c_ctx, ada_w[l], ada_b[l])

        hc = modulate(rmsnorm(ctx, norm_g[l, 0]), sc1, cc1)
        if last:
            kv_c = hc @ w_in[l][:, K_OFF:V_OFF + MIX_W]
        else:
            proj_c = hc @ w_in[l]
            kv_c = proj_c[..., K_OFF:V_OFF + MIX_W]
        s_f, s_b = ctx_states(kv_c, ret_log_decay[l])

        hx = modulate(rmsnorm(x, norm_g[l, 0]), sx1, cx1)
        mix_x = token_mix(hx @ w_in[l], ret_log_decay[l], conv_w[l], sgu_w[l], sgu_b[l],
                          w_branch[l], w_out[l], s_f, s_b, True)
        x = x + gx1 * rmsnorm(mix_x, norm_g[l, 1])
        hx = modulate(rmsnorm(x, norm_g[l, 2]), sx2, cx2)
        x = x + gx2 * rmsnorm(swiglu(hx, ffn_w_in[l], ffn_w_out[l]), norm_g[l, 3])

        if not last:
            zero_s = jnp.zeros_like(s_f)
            mix_c = token_mix(proj_c, ret_log_decay[l], conv_w[l], sgu_w[l], sgu_b[l],
                              w_branch[l], w_out[l], zero_s, zero_s, False)
            ctx = ctx + gc1 * rmsnorm(mix_c, norm_g[l, 1])
            hc = modulate(rmsnorm(ctx, norm_g[l, 2]), sc2, cc2)
            ctx = ctx + gc2 * rmsnorm(swiglu(hc, ffn_w_in[l], ffn_w_out[l]), norm_g[l, 3])
    return x
```

```python
import functools

import jax
import jax.numpy as jnp
from jax import lax
from jax.experimental import pallas as pl
from jax.experimental.pallas import tpu as pltpu

f32 = jnp.float32
bf16 = jnp.bfloat16

GRID_W = 64
MIX_W = 512
N_BRANCH = 4
GROUP_DIM = 128
N_GROUPS = MIX_W // GROUP_DIM
CHUNK = 128
ROPE_BASE = 10000.0
EPS = 1e-6
N_MIX_COLS = 10 * MIX_W
COL_F, COL_U, COL_V, COL_SB, COL_SC, COL_SX = 4, 5, 6, 7, 8, 9
MOD_ROWS = 16

VMEM_LIMIT_BYTES = 56 * 1024 * 1024


def _cparams(*sem):
    return pltpu.CompilerParams(dimension_semantics=sem, vmem_limit_bytes=VMEM_LIMIT_BYTES)


def _tile(n, pref):
    t = min(n, pref)
    while n % t:
        t //= 2
    return t


def _rms(x):
    return x * lax.rsqrt(jnp.mean(x * x, axis=-1, keepdims=True) + EPS)


def _adaln_kernel(c_ref, w_ref, b_ref, o_ref):
    s = jax.nn.silu(c_ref[...]).astype(bf16)
    o_ref[...] = jnp.dot(s, w_ref[...].astype(bf16), preferred_element_type=f32) + b_ref[...]


def _adaln(cc, ada_w, ada_b):
    depth, d, n6 = ada_w.shape
    tn = _tile(n6, 1024)
    return pl.pallas_call(
        _adaln_kernel,
        grid=(depth, n6 // tn),
        in_specs=[
            pl.BlockSpec((MOD_ROWS, d), lambda l, j: (0, 0)),
            pl.BlockSpec((None, d, tn), lambda l, j: (l, 0, j)),
            pl.BlockSpec((None, 1, tn), lambda l, j: (l, 0, j)),
        ],
        out_specs=pl.BlockSpec((None, MOD_ROWS, tn), lambda l, j: (l, 0, j)),
        out_shape=jax.ShapeDtypeStruct((depth, MOD_ROWS, n6), f32),
        compiler_params=_cparams("parallel", "parallel"),
        name="adaln",
    )(cc, ada_w, ada_b.reshape(depth, 1, n6))


class _Rows:
    def __init__(self, batch, seq, ctx_len, tm):
        self.batch, self.seq, self.ctx_len, self.tm = batch, seq, ctx_len, tm
        self.m_x, self.m_c = batch * seq, batch * ctx_len
        self.m = self.m_x + self.m_c
        assert seq % tm == 0 and self.m_c % tm == 0
        self.n_x_tiles = self.m_x // tm
        self.n_tiles = self.m // tm
        self.per_batch = seq // tm

    def mod_spec(self, layer, k, d, extra_axes=0):
        def index(i, *_):
            r = jnp.where(i < self.n_x_tiles, i // self.per_batch, self.batch)
            return ((layer * MOD_ROWS + r) * 6 + k, 0, 0)
        return pl.BlockSpec((None, 1, d), index)


def _vec_spec(d):
    return pl.BlockSpec((1, d), lambda i, *_: (0, 0))


def _norm_mod_kernel(x_ref, c_ref, g_ref, sh_ref, sc_ref, xs_ref, hx_ref, *, n_x_tiles):
    def emit(x):
        xs_ref[...] = x
        hx_ref[...] = (_rms(x) * g_ref[...] * (1.0 + sc_ref[...]) + sh_ref[...]).astype(bf16)

    @pl.when(pl.program_id(0) < n_x_tiles)
    def _():
        emit(x_ref[...])

    @pl.when(pl.program_id(0) >= n_x_tiles)
    def _():
        emit(c_ref[...])


def _norm_mod(x2d, c2d, g, mods, rows):
    d = x2d.shape[1]
    tm, nxt = rows.tm, rows.n_x_tiles
    return pl.pallas_call(
        functools.partial(_norm_mod_kernel, n_x_tiles=nxt),
        grid=(rows.n_tiles,),
        in_specs=[
            pl.BlockSpec((tm, d), lambda i: (jnp.minimum(i, nxt - 1), 0)),
            pl.BlockSpec((tm, d), lambda i: (jnp.maximum(i - nxt, 0), 0)),
            _vec_spec(d),
            rows.mod_spec(0, 0, d),
            rows.mod_spec(0, 1, d),
        ],
        out_specs=[pl.BlockSpec((tm, d), lambda i: (i, 0)), pl.BlockSpec((tm, d), lambda i: (i, 0))],
        out_shape=[jax.ShapeDtypeStruct((rows.m, d), f32), jax.ShapeDtypeStruct((rows.m, d), bf16)],
        compiler_params=_cparams("parallel"),
        name="norm_mod",
    )(x2d, c2d, g, mods, mods)


def _matmul_kernel(a_ref, b_ref, o_ref):
    o_ref[...] = jnp.dot(a_ref[...], b_ref[...], preferred_element_type=f32).astype(o_ref.dtype)


def _matmul(a, b, tm, tn, name):
    m, k = a.shape
    n = b.shape[1]
    return pl.pallas_call(
        _matmul_kernel,
        grid=(m // tm, n // tn),
        in_specs=[pl.BlockSpec((tm, k), lambda i, j: (i, 0)), pl.BlockSpec((k, tn), lambda i, j: (0, j))],
        out_specs=pl.BlockSpec((tm, tn), lambda i, j: (i, j)),
        out_shape=jax.ShapeDtypeStruct((m, n), bf16),
        compiler_params=_cparams("parallel", "arbitrary"),
        name=name,
    )(a, b)


def _rope_tables(seq):
    t = jnp.arange(seq)
    n = GROUP_DIM // 2
    freqs = ROPE_BASE ** (-jnp.arange(0, n, 2, dtype=f32) / n)

    def tab(pos):
        ang = pos.astype(f32)[:, None] * freqs[None, :]
        c, s = jnp.cos(ang), jnp.sin(ang)
        return jnp.concatenate([c, c], -1), jnp.concatenate([-s, s], -1)

    c1, s1 = tab(t // GRID_W)
    c2, s2 = tab(t % GRID_W)
    dst = jnp.arange(GROUP_DIM)
    src = jnp.where((dst % n) < n // 2, dst + n // 2, dst - n // 2)
    perm = (jnp.arange(GROUP_DIM)[:, None] == src[None, :]).astype(bf16)
    return jnp.concatenate([c1, c2], -1), jnp.concatenate([s1, s2], -1), perm


def _retention_kernel(*refs, n_chunks, rope, ctx_len):
    refs = list(refs)
    lg_ref, q_ref, k_ref, v_ref, g_ref = refs[:5]
    pos = 5
    if ctx_len:
        kc_ref, vc_ref = refs[pos:pos + 2]
        pos += 2
    if rope:
        cos_ref, sin_ref, perm_ref = refs[pos:pos + 3]
        pos += 3
    o_ref, qs, ks, sf, sb = refs[pos:]

    c_len, dh = CHUNK, GROUP_DIM
    h = pl.program_id(1)
    lgf, lgb = lg_ref[0, h], lg_ref[1, h]
    scale = dh ** -0.5
    ri = lax.broadcasted_iota(jnp.int32, (c_len, dh), 0).astype(f32)
    ci = lax.broadcasted_iota(jnp.int32, (c_len, dh), 1).astype(f32)
    rel = ri - ci
    decay = (jnp.where(rel >= 0, jnp.exp(lgf * jnp.maximum(rel, 0.0)), 0.0)
             + jnp.where(rel <= 0, jnp.exp(lgb * jnp.maximum(-rel, 0.0)), 0.0))
    kw_f = jnp.exp(lgf * (c_len - 1 - ri))
    kw_b = jnp.exp(lgb * ri)
    qw_f = jnp.exp(lgf * (ri + 1.0))
    qw_b = jnp.exp(lgb * (c_len - ri))
    cd_f = jnp.exp(jnp.full((dh, dh), lgf * c_len, f32))
    cd_b = jnp.exp(jnp.full((dh, dh), lgb * c_len, f32))

    def rows_of(c):
        return pl.ds(pl.multiple_of(c * c_len, c_len), c_len)

    def kt_dot(kb, vw):
        return lax.dot_general(kb, vw, (((0,), (0,)), ((), ())), preferred_element_type=f32)

    def prep(c, carry):
        rows = rows_of(c)
        q, k = q_ref[rows, :], k_ref[rows, :]
        qf, kf = q.astype(f32), k.astype(f32)
        if rope:
            cos, sin = cos_ref[rows, :], sin_ref[rows, :]
            qf = qf * cos + jnp.dot(q, perm_ref[...], preferred_element_type=f32) * sin
            kf = kf * cos + jnp.dot(k, perm_ref[...], preferred_element_type=f32) * sin
        kb = (kf * scale).astype(bf16)
        qs[rows, :] = qf.astype(bf16)
        ks[rows, :] = kb
        v = v_ref[rows, :].astype(f32)
        vw = jnp.concatenate([v * kw_f, v * kw_b], axis=1).astype(bf16)
        ds = kt_dot(kb, vw)
        sf[c] = ds[:, :dh]
        sb[c] = ds[:, dh:]
        return carry

    lax.fori_loop(0, n_chunks, prep, 0)

    if ctx_len:
        mi = lax.broadcasted_iota(jnp.int32, (ctx_len, dh), 0).astype(f32)
        kcb = (kc_ref[...].astype(f32) * scale).astype(bf16)
        vc = vc_ref[...].astype(f32)
        vcw = jnp.concatenate([vc * jnp.exp(lgf * (ctx_len - 1 - mi)), vc * jnp.exp(lgb * mi)],
                              axis=1).astype(bf16)
        s0 = kt_dot(kcb, vcw)
        s0_f, s0_b = s0[:, :dh], s0[:, dh:]
    else:
        s0_f = s0_b = jnp.zeros((dh, dh), f32)

    def scan_f(c, s):
        nxt = cd_f * s + sf[c]
        sf[c] = s
        return nxt

    def scan_b(j, s):
        c = n_chunks - 1 - j
        nxt = cd_b * s + sb[c]
        sb[c] = s
        return nxt

    lax.fori_loop(0, n_chunks, scan_f, s0_f)
    lax.fori_loop(0, n_chunks, scan_b, s0_b)

    def emit(c, carry):
        rows = rows_of(c)
        q, k, v = qs[rows, :], ks[rows, :], v_ref[rows, :]
        sc = lax.dot_general(q, k, (((1,), (1,)), ((), ())), preferred_element_type=f32) * decay
        inner = jnp.dot(sc.astype(bf16), v, preferred_element_type=f32)
        s2 = jnp.concatenate([sf[c], sb[c]], axis=1).astype(bf16)
        cr = jnp.dot(q, s2, preferred_element_type=f32)
        ret = inner + qw_f * cr[:, :dh] + qw_b * cr[:, dh:]
        g = g_ref[rows, :].astype(f32)
        o_ref[rows, :] = (jax.nn.silu(g) * _rms(ret)).astype(bf16)
        return carry

    lax.fori_loop(0, n_chunks, emit, 0)


def _retention(proj, log_decay, rows, rope_tabs, prev_out):
    latent = prev_out is None
    seq = rows.seq if latent else rows.ctx_len
    n_chunks = seq // CHUNK
    row0 = 0 if latent else rows.m_x // seq
    hq = MIX_W // GROUP_DIM

    def col(off):
        return pl.BlockSpec((seq, GROUP_DIM), lambda b, h: (row0 + b, off * hq + h))

    in_specs = [pl.BlockSpec(memory_space=pltpu.SMEM), col(0), col(1), col(2), col(3)]
    args = [log_decay, proj, proj, proj, proj]
    if latent:
        cb0 = rows.m_x // rows.ctx_len
        for off in (1, 2):
            in_specs.append(pl.BlockSpec((rows.ctx_len, GROUP_DIM),
                                         lambda b, h, off=off: (cb0 + b, off * hq + h)))
            args.append(proj)
        cos, sin, perm = rope_tabs
        in_specs += [pl.BlockSpec((seq, GROUP_DIM), lambda b, h: (0, 0)),
                     pl.BlockSpec((seq, GROUP_DIM), lambda b, h: (0, 0)),
                     pl.BlockSpec((GROUP_DIM, GROUP_DIM), lambda b, h: (0, 0))]
        args += [cos, sin, perm]
        aliases = {}
    else:
        in_specs.append(pl.BlockSpec(memory_space=pl.ANY))
        args.append(prev_out)
        aliases = {len(args) - 1: 0}

    kernel = functools.partial(_retention_kernel, n_chunks=n_chunks, rope=latent,
                               ctx_len=rows.ctx_len if latent else 0)
    if not latent:
        def kernel(*refs, _k=kernel):
            return _k(*refs[:5], *refs[6:])

    return pl.pallas_call(
        kernel,
        grid=(rows.batch, hq),
        in_specs=in_specs,
        out_specs=pl.BlockSpec((seq, GROUP_DIM), lambda b, h: (row0 + b, h)),
        out_shape=jax.ShapeDtypeStruct((rows.m, MIX_W), bf16),
        scratch_shapes=[pltpu.VMEM((seq, GROUP_DIM), bf16), pltpu.VMEM((seq, GROUP_DIM), bf16),
                        pltpu.VMEM((n_chunks, GROUP_DIM, GROUP_DIM), f32),
                        pltpu.VMEM((n_chunks, GROUP_DIM, GROUP_DIM), f32)],
        input_output_aliases=aliases,
        compiler_params=_cparams("parallel", "parallel"),
        name="retention_x" if latent else "retention_ctx",
    )(*args)


def _dft_tables(seq):
    def cs(n):
        i = jnp.arange(n, dtype=jnp.int32)
        ang = ((i[:, None] * i[None, :]) % n).astype(f32) * (2.0 * jnp.pi / n)
        return jnp.cos(ang), jnp.sin(ang)

    cc, sc = cs(GROUP_DIM)
    cl, sl = cs(seq)
    return jnp.concatenate([cc, sc], 1).astype(bf16), jnp.concatenate([cl, -sl], 1).astype(bf16)


def _fourier_kernel(z_ref, ch_ref, pos_ref, o_ref, zz, *, seq):
    @pl.when(pl.program_id(1) == 0)
    def _():
        norm = (seq * GROUP_DIM) ** -0.5
        for g in range(N_GROUPS):
            cols = slice(g * GROUP_DIM, (g + 1) * GROUP_DIM)
            t = jnp.dot(z_ref[:, cols], ch_ref[...], preferred_element_type=f32) * norm
            zz[0:seq, cols] = t[:, :GROUP_DIM].astype(bf16)
            zz[seq:2 * seq, cols] = t[:, GROUP_DIM:].astype(bf16)

    o_ref[...] = jnp.dot(pos_ref[...], zz[...], preferred_element_type=f32).astype(bf16)


def _fourier(proj, rows, tables, prev_out):
    latent = prev_out is None
    seq = rows.seq if latent else rows.ctx_len
    ch_tab, pos_tab = tables
    tm = _tile(seq, 512)
    n_mt = seq // tm
    row0 = 0 if latent else rows.m_x // seq
    in_specs = [pl.BlockSpec((seq, MIX_W), lambda b, m: (row0 + b, COL_F)),
                pl.BlockSpec((GROUP_DIM, 2 * GROUP_DIM), lambda b, m: (0, 0)),
                pl.BlockSpec((tm, 2 * seq), lambda b, m: (m, 0))]
    args = [proj, ch_tab, pos_tab]
    kernel = functools.partial(_fourier_kernel, seq=seq)
    aliases = {}
    if not latent:
        in_specs.append(pl.BlockSpec(memory_space=pl.ANY))
        args.append(prev_out)
        aliases = {3: 0}

        def kernel(z_ref, ch_ref, pos_ref, _carrier, o_ref, zz, _k=kernel):
            return _k(z_ref, ch_ref, pos_ref, o_ref, zz)

    return pl.pallas_call(
        kernel,
        grid=(rows.batch, n_mt),
        in_specs=in_specs,
        out_specs=pl.BlockSpec((tm, MIX_W), lambda b, m: ((row0 + b) * n_mt + m, 0)),
        out_shape=jax.ShapeDtypeStruct((rows.m, MIX_W), bf16),
        scratch_shapes=[pltpu.VMEM((2 * seq, MIX_W), bf16)],
        input_output_aliases=aliases,
        compiler_params=_cparams("parallel", "arbitrary"),
        name="fourier_x" if latent else "fourier_ctx",
    )(*args)


def _sgu_kernel(u_ref, v_ref, w_ref, b_ref, o_ref, *, tm):
    for ch in range(tm // CHUNK):
        rows = slice(ch * CHUNK, (ch + 1) * CHUNK)
        u = jax.nn.gelu(u_ref[rows, :].astype(f32))
        v = jax.nn.gelu(v_ref[rows, :].astype(f32))
        outs = []
        for g in range(N_GROUPS):
            cols = slice(g * GROUP_DIM, (g + 1) * GROUP_DIM)
            vg = v[:, cols]
            dv = vg - jnp.mean(vg, axis=-1, keepdims=True)
            vn = dv * lax.rsqrt(jnp.mean(dv * dv, axis=-1, keepdims=True) + EPS)
            s = jnp.dot(w_ref[g], vn.astype(bf16), preferred_element_type=f32) + b_ref[:, cols]
            outs.append(u[:, cols] * s)
        o_ref[rows, :] = jnp.concatenate(outs, axis=1).astype(bf16)


def _sgu(proj, sgu_w, sgu_bias, rows, tm):
    return pl.pallas_call(
        functools.partial(_sgu_kernel, tm=tm),
        grid=(rows.m // tm,),
        in_specs=[pl.BlockSpec((tm, MIX_W), lambda i: (i, COL_U)),
                  pl.BlockSpec((tm, MIX_W), lambda i: (i, COL_V)),
                  pl.BlockSpec((N_GROUPS, CHUNK, CHUNK), lambda i: (0, 0, 0)),
                  pl.BlockSpec((CHUNK, MIX_W), lambda i: (0, 0))],
        out_specs=pl.BlockSpec((tm, MIX_W), lambda i: (i, 0)),
        out_shape=jax.ShapeDtypeStruct((rows.m, MIX_W), bf16),
        compiler_params=_cparams("parallel"),
        name="sgu",
    )(proj, proj, sgu_w, sgu_bias)


def _conv_kernel(b_ref, c_ref, x_ref, w_ref, o_ref, *, tm, n_x_tiles, ctx_len):
    y = c_ref[...].astype(f32) * x_ref[...].astype(f32)
    r = lax.broadcasted_iota(jnp.int32, y.shape, 0)
    period = jnp.where(pl.program_id(0) < n_x_tiles, GRID_W, ctx_len)
    rp = r & (period - 1)
    prev = jnp.where(rp == 0, 0.0, pltpu.roll(y, 1, axis=0))
    nxt = jnp.where(rp == period - 1, 0.0, pltpu.roll(y, tm - 1, axis=0))
    conv = w_ref[0:1, :] * prev + w_ref[1:2, :] * y + w_ref[2:3, :] * nxt
    o_ref[...] = (b_ref[...].astype(f32) * conv).astype(bf16)


def _conv(proj, conv_w, rows, tm):
    assert GRID_W & (GRID_W - 1) == 0 and rows.ctx_len & (rows.ctx_len - 1) == 0
    assert tm % GRID_W == 0 and tm % rows.ctx_len == 0
    return pl.pallas_call(
        functools.partial(_conv_kernel, tm=tm, n_x_tiles=rows.m_x // tm, ctx_len=rows.ctx_len),
        grid=(rows.m // tm,),
        in_specs=[pl.BlockSpec((tm, MIX_W), lambda i: (i, COL_SB)),
                  pl.BlockSpec((tm, MIX_W), lambda i: (i, COL_SC)),
                  pl.BlockSpec((tm, MIX_W), lambda i: (i, COL_SX)),
                  pl.BlockSpec((3, MIX_W), lambda i: (0, 0))],
        out_specs=pl.BlockSpec((tm, MIX_W), lambda i: (i, 0)),
        out_shape=jax.ShapeDtypeStruct((rows.m, MIX_W), bf16),
        compiler_params=_cparams("parallel"),
        name="conv",
    )(proj, proj, proj, conv_w)


def _merge_kernel(hx_ref, ya_ref, yb_ref, yc_ref, yd_ref, wg_ref, wb_ref, o_ref, acc):
    n = pl.program_id(2)
    for k, y_ref in enumerate((ya_ref, yb_ref, yc_ref, yd_ref)):
        @pl.when(n == k)
        def _(k=k, y_ref=y_ref):
            gate = jax.nn.sigmoid(jnp.dot(hx_ref[...], wg_ref[...], preferred_element_type=f32))
            term = gate * jnp.dot(y_ref[...], wb_ref[...], preferred_element_type=f32)
            if k == 0:
                acc[...] = term
            elif k < N_BRANCH - 1:
                acc[...] += term
            else:
                o_ref[...] = (acc[...] + term).astype(bf16)


def _merge(hx, ys, w_gate, w_branch, tm, tn):
    m, d = hx.shape
    nj = d // tn
    y_spec = pl.BlockSpec((tm, MIX_W), lambda i, j, n: (i, 0))
    return pl.pallas_call(
        _merge_kernel,
        grid=(m // tm, nj, N_BRANCH),
        in_specs=[pl.BlockSpec((tm, d), lambda i, j, n: (i, 0)), y_spec, y_spec, y_spec, y_spec,
                  pl.BlockSpec((d, tn), lambda i, j, n: (0, n * nj + j)),
                  pl.BlockSpec((None, MIX_W, tn), lambda i, j, n: (n, 0, j))],
        out_specs=pl.BlockSpec((tm, tn), lambda i, j, n: (i, j)),
        out_shape=jax.ShapeDtypeStruct((m, d), bf16),
        scratch_shapes=[pltpu.VMEM((tm, tn), f32)],
        compiler_params=_cparams("parallel", "arbitrary", "arbitrary"),
        name="merge",
    )(hx, *ys, w_gate, w_branch)


def _outproj_kernel(m_ref, w_ref, x_ref, g1_ref, g2_ref, gate_ref, sh_ref, sc_ref, xo_ref, ho_ref):
    mix = jnp.dot(m_ref[...], w_ref[...], preferred_element_type=f32)
    xn = x_ref[...] + gate_ref[...] * (_rms(mix) * g1_ref[...])
    xo_ref[...] = xn
    ho_ref[...] = (_rms(xn) * g2_ref[...] * (1.0 + sc_ref[...]) + sh_ref[...]).astype(bf16)


def _outproj(merged, w_out, xs, g1, g2, mods, layer, rows):
    d = xs.shape[1]
    tm = rows.tm
    row = pl.BlockSpec((tm, d), lambda i: (i, 0))
    return pl.pallas_call(
        _outproj_kernel,
        grid=(rows.n_tiles,),
        in_specs=[row, pl.BlockSpec((d, d), lambda i: (0, 0)), row, _vec_spec(d), _vec_spec(d),
                  rows.mod_spec(layer, 2, d), rows.mod_spec(layer, 3, d), rows.mod_spec(layer, 4, d)],
        out_specs=[row, row],
        out_shape=[jax.ShapeDtypeStruct((rows.m, d), f32), jax.ShapeDtypeStruct((rows.m, d), bf16)],
        compiler_params=_cparams("parallel"),
        name="outproj",
    )(merged, w_out, xs, g1, g2, mods, mods, mods)


def _ffn_kernel(*refs, n_f, has_next):
    if has_next:
        h_ref, wa_ref, wb_ref, w2_ref, x_ref, g3_ref, gate_ref, gn_ref, sh_ref, sc_ref, xo_ref, ho_ref = refs
    else:
        h_ref, wa_ref, wb_ref, w2_ref, x_ref, g3_ref, gate_ref, xo_ref = refs
    f = pl.program_id(1)
    h = h_ref[...]
    a = jnp.dot(h, wa_ref[...], preferred_element_type=f32)
    b = jnp.dot(h, wb_ref[...], preferred_element_type=f32)
    p = jnp.dot((jax.nn.silu(a) * b).astype(bf16), w2_ref[...], preferred_element_type=f32)

    @pl.when(f == 0)
    def _():
        xo_ref[...] = p

    @pl.when(f > 0)
    def _():
        xo_ref[...] += p

    @pl.when(f == n_f - 1)
    def _():
        xn = x_ref[...] + gate_ref[...] * (_rms(xo_ref[...]) * g3_ref[...])
        xo_ref[...] = xn
        if has_next:
            ho_ref[...] = (_rms(xn) * gn_ref[...] * (1.0 + sc_ref[...]) + sh_ref[...]).astype(bf16)


def _ffn(hx, w1, w2, xs, g3, g_next, mods, layer, rows, tf):
    d = xs.shape[1]
    d_ff = w2.shape[0]
    n_f = d_ff // tf
    tm = rows.tm
    has_next = g_next is not None
    row = pl.BlockSpec((tm, d), lambda i, f: (i, 0))
    in_specs = [row,
                pl.BlockSpec((d, tf), lambda i, f: (0, f)),
                pl.BlockSpec((d, tf), lambda i, f: (0, n_f + f)),
                pl.BlockSpec((tf, d), lambda i, f: (f, 0)),
                row, _vec_spec(d), rows.mod_spec(layer, 5, d)]
    args = [hx, w1, w1, w2, xs, g3, mods]
    out_specs = [row]
    out_shape = [jax.ShapeDtypeStruct((rows.m, d), f32)]
    if has_next:
        in_specs += [_vec_spec(d), rows.mod_spec(layer + 1, 0, d), rows.mod_spec(layer + 1, 1, d)]
        args += [g_next, mods, mods]
        out_specs.append(row)
        out_shape.append(jax.ShapeDtypeStruct((rows.m, d), bf16))
    out = pl.pallas_call(
        functools.partial(_ffn_kernel, n_f=n_f, has_next=has_next),
        grid=(rows.n_tiles, n_f),
        in_specs=in_specs,
        out_specs=out_specs,
        out_shape=out_shape,
        compiler_params=_cparams("parallel", "arbitrary"),
        name="ffn",
    )(*args)
    return out if has_next else (out[0], None)


def kernel(x, c, ctx, c_ctx, ada_w, ada_b, norm_g, w_in, ret_log_decay, conv_w, sgu_w, sgu_b,
           w_branch, w_out, ffn_w_in, ffn_w_out):
    batch, seq, d = x.shape
    ctx_len = ctx.shape[1]
    depth = ada_w.shape[0]
    d_ff = ffn_w_out.shape[1]
    assert batch + 1 <= MOD_ROWS and w_in.shape[2] == N_MIX_COLS + N_BRANCH * d

    rows = _Rows(batch, seq, ctx_len, _tile(batch * ctx_len, 512))
    tm_big = _tile(batch * ctx_len, 1024)

    cc = jnp.zeros((MOD_ROWS, d), f32).at[:batch].set(c).at[batch].set(c_ctx)
    mods = _adaln(cc, ada_w, ada_b).reshape(depth * MOD_ROWS * 6, 1, d)

    rope_tabs = _rope_tables(seq)
    dft_x = _dft_tables(seq)
    dft_c = _dft_tables(ctx_len)

    xs, hx = _norm_mod(x.reshape(batch * seq, d), ctx.reshape(batch * ctx_len, d),
                       norm_g[0, 0:1], mods, rows)

    for l in range(depth):
        w_mix = w_in[l, :, :N_MIX_COLS].astype(bf16)
        w_gate = w_in[l, :, N_MIX_COLS:].astype(bf16)
        sgu_bias = jnp.repeat(jnp.transpose(sgu_b[l]), GROUP_DIM, axis=1)

        proj = _matmul(hx, w_mix, tm_big, _tile(N_MIX_COLS, 1024), "inproj")
        ya = _retention(proj, ret_log_decay[l], rows, rope_tabs, None)
        ya = _retention(proj, ret_log_decay[l], rows, None, ya)
        yb = _fourier(proj, rows, dft_x, None)
        yb = _fourier(proj, rows, dft_c, yb)
        yc = _sgu(proj, sgu_w[l].astype(bf16), sgu_bias, rows, rows.tm)
        yd = _conv(proj, conv_w[l], rows, rows.tm)
        merged = _merge(hx, (ya, yb, yc, yd), w_gate, w_branch[l].astype(bf16), tm_big, _tile(d, 1024))
        xs, hx2 = _outproj(merged, w_out[l].astype(bf16), xs, norm_g[l, 1:2], norm_g[l, 2:3], mods, l, rows)
        g_next = norm_g[l + 1, 0:1] if l + 1 < depth else None
        xs, hx = _ffn(hx2, ffn_w_in[l].astype(bf16), ffn_w_out[l].astype(bf16), xs, norm_g[l, 3:4],
                      g_next, mods, l, rows, _tile(d_ff, 512))

    return xs[:batch * seq].reshape(batch, seq, d)
```

```python
import functools

import jax
import jax.numpy as jnp
from jax import lax
from jax.experimental import pallas as pl
from jax.experimental.pallas import tpu as pltpu

f32 = jnp.float32
bf16 = jnp.bfloat16

GRID_W = 64
MIX_W = 512
N_BRANCH = 4
GROUP_DIM = 128
N_GROUPS = MIX_W // GROUP_DIM
CHUNK = 128
ROPE_BASE = 10000.0
EPS = 1e-6
N_MIX_COLS = 10 * MIX_W
COL_F, COL_U, COL_V, COL_SB, COL_SC, COL_SX = 4, 5, 6, 7, 8, 9
MOD_ROWS = 16
CHUNK_UNROLL = 4
FFN_ROW_CHUNK = 256
MERGE_ROW_CHUNK = 512
OUTPROJ_ROW_CHUNK = 256

VMEM_LIMIT_BYTES = 56 * 1024 * 1024


def _cparams(*sem):
    return pltpu.CompilerParams(dimension_semantics=sem, vmem_limit_bytes=VMEM_LIMIT_BYTES)


def _tile(n, pref):
    t = min(n, pref)
    while n % t:
        t //= 2
    return t


def _rms(x):
    return x * lax.rsqrt(jnp.mean(x * x, axis=-1, keepdims=True) + EPS)


def _adaln_kernel(c_ref, w_ref, b_ref, o_ref):
    s = jax.nn.silu(c_ref[...]).astype(bf16)
    o_ref[...] = jnp.dot(s, w_ref[...].astype(bf16), preferred_element_type=f32) + b_ref[...]


def _adaln(cc, ada_w, ada_b):
    depth, d, n6 = ada_w.shape
    tn = _tile(n6, 1024)
    return pl.pallas_call(
        _adaln_kernel,
        grid=(depth, n6 // tn),
        in_specs=[
            pl.BlockSpec((MOD_ROWS, d), lambda l, j: (0, 0)),
            pl.BlockSpec((None, d, tn), lambda l, j: (l, 0, j)),
            pl.BlockSpec((None, 1, tn), lambda l, j: (l, 0, j)),
        ],
        out_specs=pl.BlockSpec((None, MOD_ROWS, tn), lambda l, j: (l, 0, j)),
        out_shape=jax.ShapeDtypeStruct((depth, MOD_ROWS, n6), f32),
        compiler_params=_cparams("parallel", "parallel"),
        name="adaln",
    )(cc, ada_w, ada_b.reshape(depth, 1, n6))


class _Rows:
    def __init__(self, batch, seq, ctx_len, tm):
        self.batch, self.seq, self.ctx_len, self.tm = batch, seq, ctx_len, tm
        self.m_x, self.m_c = batch * seq, batch * ctx_len
        self.m = self.m_x + self.m_c
        assert seq % tm == 0 and self.m_c % tm == 0
        self.n_x_tiles = self.m_x // tm
        self.n_tiles = self.m // tm
        self.per_batch = seq // tm

    def mod_spec(self, layer, k, d, extra_axes=0):
        def index(i, *_):
            r = jnp.where(i < self.n_x_tiles, i // self.per_batch, self.batch)
            return ((layer * MOD_ROWS + r) * 6 + k, 0, 0)
        return pl.BlockSpec((None, 1, d), index)


def _vec_spec(d):
    return pl.BlockSpec((1, d), lambda i, *_: (0, 0))


def _norm_mod_kernel(x_ref, c_ref, g_ref, sh_ref, sc_ref, xs_ref, hx_ref, *, n_x_tiles):
    def emit(x):
        xs_ref[...] = x
        hx_ref[...] = (_rms(x) * g_ref[...] * (1.0 + sc_ref[...]) + sh_ref[...]).astype(bf16)

    @pl.when(pl.program_id(0) < n_x_tiles)
    def _():
        emit(x_ref[...])

    @pl.when(pl.program_id(0) >= n_x_tiles)
    def _():
        emit(c_ref[...])


def _norm_mod(x2d, c2d, g, mods, rows):
    d = x2d.shape[1]
    tm, nxt = rows.tm, rows.n_x_tiles
    return pl.pallas_call(
        functools.partial(_norm_mod_kernel, n_x_tiles=nxt),
        grid=(rows.n_tiles,),
        in_specs=[
            pl.BlockSpec((tm, d), lambda i: (jnp.minimum(i, nxt - 1), 0)),
            pl.BlockSpec((tm, d), lambda i: (jnp.maximum(i - nxt, 0), 0)),
            _vec_spec(d),
            rows.mod_spec(0, 0, d),
            rows.mod_spec(0, 1, d),
        ],
        out_specs=[pl.BlockSpec((tm, d), lambda i: (i, 0)), pl.BlockSpec((tm, d), lambda i: (i, 0))],
        out_shape=[jax.ShapeDtypeStruct((rows.m, d), f32), jax.ShapeDtypeStruct((rows.m, d), bf16)],
        compiler_params=_cparams("parallel"),
        name="norm_mod",
    )(x2d, c2d, g, mods, mods)


def _matmul_kernel(a_ref, b_ref, o_ref):
    o_ref[...] = jnp.dot(a_ref[...], b_ref[...], preferred_element_type=f32).astype(o_ref.dtype)


def _matmul(a, b, tm, tn, name):
    m, k = a.shape
    n = b.shape[1]
    return pl.pallas_call(
        _matmul_kernel,
        grid=(m // tm, n // tn),
        in_specs=[pl.BlockSpec((tm, k), lambda i, j: (i, 0)), pl.BlockSpec((k, tn), lambda i, j: (0, j))],
        out_specs=pl.BlockSpec((tm, tn), lambda i, j: (i, j)),
        out_shape=jax.ShapeDtypeStruct((m, n), bf16),
        compiler_params=_cparams("parallel", "arbitrary"),
        name=name,
    )(a, b)


def _rope_tables(seq):
    t = jnp.arange(seq)
    n = GROUP_DIM // 2
    freqs = ROPE_BASE ** (-jnp.arange(0, n, 2, dtype=f32) / n)

    def tab(pos):
        ang = pos.astype(f32)[:, None] * freqs[None, :]
        c, s = jnp.cos(ang), jnp.sin(ang)
        return jnp.concatenate([c, c], -1), jnp.concatenate([-s, s], -1)

    c1, s1 = tab(t // GRID_W)
    c2, s2 = tab(t % GRID_W)
    dst = jnp.arange(GROUP_DIM)
    src = jnp.where((dst % n) < n // 2, dst + n // 2, dst - n // 2)
    perm = (jnp.arange(GROUP_DIM)[:, None] == src[None, :]).astype(bf16)
    return jnp.concatenate([c1, c2], -1), jnp.concatenate([s1, s2], -1), perm


def _retention_kernel(*refs, n_chunks, rope, ctx_len):
    refs = list(refs)
    lg_ref, q_ref, k_ref, v_ref, g_ref = refs[:5]
    pos = 5
    if ctx_len:
        kc_ref, vc_ref = refs[pos:pos + 2]
        pos += 2
    if rope:
        cos_ref, sin_ref, perm_ref = refs[pos:pos + 3]
        pos += 3
    o_ref, qs, ks, sf, sb = refs[pos:]

    c_len, dh = CHUNK, GROUP_DIM
    h = pl.program_id(1)
    lgf, lgb = lg_ref[0, h], lg_ref[1, h]
    scale = dh ** -0.5
    ri = lax.broadcasted_iota(jnp.int32, (c_len, dh), 0).astype(f32)
    ci = lax.broadcasted_iota(jnp.int32, (c_len, dh), 1).astype(f32)
    rel = ri - ci
    decay = (jnp.where(rel >= 0, jnp.exp(lgf * jnp.maximum(rel, 0.0)), 0.0)
             + jnp.where(rel <= 0, jnp.exp(lgb * jnp.maximum(-rel, 0.0)), 0.0))
    kw_f = jnp.exp(lgf * (c_len - 1 - ri))
    kw_b = jnp.exp(lgb * ri)
    qw_f = jnp.exp(lgf * (ri + 1.0))
    qw_b = jnp.exp(lgb * (c_len - ri))
    cd_f = jnp.exp(jnp.full((dh, dh), lgf * c_len, f32))
    cd_b = jnp.exp(jnp.full((dh, dh), lgb * c_len, f32))

    def rows_of(c):
        return pl.ds(pl.multiple_of(c * c_len, c_len), c_len)

    def kt_dot(kb, vw):
        return lax.dot_general(kb, vw, (((0,), (0,)), ((), ())), preferred_element_type=f32)

    def prep(c, carry):
        rows = rows_of(c)
        q, k = q_ref[rows, :], k_ref[rows, :]
        qf, kf = q.astype(f32), k.astype(f32)
        if rope:
            cos, sin = cos_ref[rows, :], sin_ref[rows, :]
            qf = qf * cos + jnp.dot(q, perm_ref[...], preferred_element_type=f32) * sin
            kf = kf * cos + jnp.dot(k, perm_ref[...], preferred_element_type=f32) * sin
        kb = (kf * scale).astype(bf16)
        qs[rows, :] = qf.astype(bf16)
        ks[rows, :] = kb
        v = v_ref[rows, :].astype(f32)
        vw = jnp.concatenate([v * kw_f, v * kw_b], axis=1).astype(bf16)
        ds = kt_dot(kb, vw)
        sf[c] = ds[:, :dh]
        sb[c] = ds[:, dh:]
        return carry

    unroll = min(n_chunks, CHUNK_UNROLL)
    lax.fori_loop(0, n_chunks, prep, 0, unroll=unroll)

    if ctx_len:
        mi = lax.broadcasted_iota(jnp.int32, (ctx_len, dh), 0).astype(f32)
        kcb = (kc_ref[...].astype(f32) * scale).astype(bf16)
        vc = vc_ref[...].astype(f32)
        vcw = jnp.concatenate([vc * jnp.exp(lgf * (ctx_len - 1 - mi)), vc * jnp.exp(lgb * mi)],
                              axis=1).astype(bf16)
        s0 = kt_dot(kcb, vcw)
        s0_f, s0_b = s0[:, :dh], s0[:, dh:]
    else:
        s0_f = s0_b = jnp.zeros((dh, dh), f32)

    def scan_f(c, s):
        nxt = cd_f * s + sf[c]
        sf[c] = s
        return nxt

    def scan_b(j, s):
        c = n_chunks - 1 - j
        nxt = cd_b * s + sb[c]
        sb[c] = s
        return nxt

    lax.fori_loop(0, n_chunks, scan_f, s0_f)
    lax.fori_loop(0, n_chunks, scan_b, s0_b)

    def emit(c, carry):
        rows = rows_of(c)
        q, k, v = qs[rows, :], ks[rows, :], v_ref[rows, :]
        sc = lax.dot_general(q, k, (((1,), (1,)), ((), ())), preferred_element_type=f32) * decay
        inner = jnp.dot(sc.astype(bf16), v, preferred_element_type=f32)
        s2 = jnp.concatenate([sf[c], sb[c]], axis=1).astype(bf16)
        cr = jnp.dot(q, s2, preferred_element_type=f32)
        ret = inner + qw_f * cr[:, :dh] + qw_b * cr[:, dh:]
        g = g_ref[rows, :].astype(f32)
        o_ref[rows, :] = (jax.nn.silu(g) * _rms(ret)).astype(bf16)
        return carry

    lax.fori_loop(0, n_chunks, emit, 0, unroll=unroll)


def _retention(proj, log_decay, rows, rope_tabs, prev_out):
    latent = prev_out is None
    seq = rows.seq if latent else rows.ctx_len
    n_chunks = seq // CHUNK
    row0 = 0 if latent else rows.m_x // seq
    hq = MIX_W // GROUP_DIM

    def col(off):
        return pl.BlockSpec((seq, GROUP_DIM), lambda b, h: (row0 + b, off * hq + h))

    in_specs = [pl.BlockSpec(memory_space=pltpu.SMEM), col(0), col(1), col(2), col(3)]
    args = [log_decay, proj, proj, proj, proj]
    if latent:
        cb0 = rows.m_x // rows.ctx_len
        for off in (1, 2):
            in_specs.append(pl.BlockSpec((rows.ctx_len, GROUP_DIM),
                                         lambda b, h, off=off: (cb0 + b, off * hq + h)))
            args.append(proj)
        cos, sin, perm = rope_tabs
        in_specs += [pl.BlockSpec((seq, GROUP_DIM), lambda b, h: (0, 0)),
                     pl.BlockSpec((seq, GROUP_DIM), lambda b, h: (0, 0)),
                     pl.BlockSpec((GROUP_DIM, GROUP_DIM), lambda b, h: (0, 0))]
        args += [cos, sin, perm]
        aliases = {}
    else:
        in_specs.append(pl.BlockSpec(memory_space=pl.ANY))
        args.append(prev_out)
        aliases = {len(args) - 1: 0}

    kernel = functools.partial(_retention_kernel, n_chunks=n_chunks, rope=latent,
                               ctx_len=rows.ctx_len if latent else 0)
    if not latent:
        def kernel(*refs, _k=kernel):
            return _k(*refs[:5], *refs[6:])

    return pl.pallas_call(
        kernel,
        grid=(rows.batch, hq),
        in_specs=in_specs,
        out_specs=pl.BlockSpec((seq, GROUP_DIM), lambda b, h: (row0 + b, h)),
        out_shape=jax.ShapeDtypeStruct((rows.m, MIX_W), bf16),
        scratch_shapes=[pltpu.VMEM((seq, GROUP_DIM), bf16), pltpu.VMEM((seq, GROUP_DIM), bf16),
                        pltpu.VMEM((n_chunks, GROUP_DIM, GROUP_DIM), f32),
                        pltpu.VMEM((n_chunks, GROUP_DIM, GROUP_DIM), f32)],
        input_output_aliases=aliases,
        compiler_params=_cparams("parallel", "parallel"),
        name="retention_x" if latent else "retention_ctx",
    )(*args)


def _dft_tables(seq):
    def cs(n):
        i = jnp.arange(n, dtype=jnp.int32)
        ang = ((i[:, None] * i[None, :]) % n).astype(f32) * (2.0 * jnp.pi / n)
        return jnp.cos(ang), jnp.sin(ang)

    cc, sc = cs(GROUP_DIM)
    cl, sl = cs(seq)
    return jnp.concatenate([cc, sc], 1).astype(bf16), jnp.concatenate([cl, -sl], 1).astype(bf16)


def _fourier_kernel(z_ref, ch_ref, pos_ref, o_ref, zz, *, seq):
    @pl.when(pl.program_id(1) == 0)
    def _():
        norm = (seq * GROUP_DIM) ** -0.5
        for g in range(N_GROUPS):
            cols = slice(g * GROUP_DIM, (g + 1) * GROUP_DIM)
            t = jnp.dot(z_ref[:, cols], ch_ref[...], preferred_element_type=f32) * norm
            zz[0:seq, cols] = t[:, :GROUP_DIM].astype(bf16)
            zz[seq:2 * seq, cols] = t[:, GROUP_DIM:].astype(bf16)

    o_ref[...] = jnp.dot(pos_ref[...], zz[...], preferred_element_type=f32).astype(bf16)


def _fourier(proj, rows, tables, prev_out):
    latent = prev_out is None
    seq = rows.seq if latent else rows.ctx_len
    ch_tab, pos_tab = tables
    tm = _tile(seq, 512)
    n_mt = seq // tm
    row0 = 0 if latent else rows.m_x // seq
    in_specs = [pl.BlockSpec((seq, MIX_W), lambda b, m: (row0 + b, COL_F)),
                pl.BlockSpec((GROUP_DIM, 2 * GROUP_DIM), lambda b, m: (0, 0)),
                pl.BlockSpec((tm, 2 * seq), lambda b, m: (m, 0))]
    args = [proj, ch_tab, pos_tab]
    kernel = functools.partial(_fourier_kernel, seq=seq)
    aliases = {}
    if not latent:
        in_specs.append(pl.BlockSpec(memory_space=pl.ANY))
        args.append(prev_out)
        aliases = {3: 0}

        def kernel(z_ref, ch_ref, pos_ref, _carrier, o_ref, zz, _k=kernel):
            return _k(z_ref, ch_ref, pos_ref, o_ref, zz)

    return pl.pallas_call(
        kernel,
        grid=(rows.batch, n_mt),
        in_specs=in_specs,
        out_specs=pl.BlockSpec((tm, MIX_W), lambda b, m: ((row0 + b) * n_mt + m, 0)),
        out_shape=jax.ShapeDtypeStruct((rows.m, MIX_W), bf16),
        scratch_shapes=[pltpu.VMEM((2 * seq, MIX_W), bf16)],
        input_output_aliases=aliases,
        compiler_params=_cparams("parallel", "arbitrary"),
        name="fourier_x" if latent else "fourier_ctx",
    )(*args)


def _sgu_kernel(u_ref, v_ref, w_ref, b_ref, o_ref, *, tm):
    for ch in range(tm // CHUNK):
        rows = slice(ch * CHUNK, (ch + 1) * CHUNK)
        u = jax.nn.gelu(u_ref[rows, :].astype(f32))
        v = jax.nn.gelu(v_ref[rows, :].astype(f32))
        outs = []
        for g in range(N_GROUPS):
            cols = slice(g * GROUP_DIM, (g + 1) * GROUP_DIM)
            vg = v[:, cols]
            dv = vg - jnp.mean(vg, axis=-1, keepdims=True)
            vn = dv * lax.rsqrt(jnp.mean(dv * dv, axis=-1, keepdims=True) + EPS)
            s = jnp.dot(w_ref[g], vn.astype(bf16), preferred_element_type=f32) + b_ref[:, cols]
            outs.append(u[:, cols] * s)
        o_ref[rows, :] = jnp.concatenate(outs, axis=1).astype(bf16)


def _sgu(proj, sgu_w, sgu_bias, rows, tm, n_rows):
    return pl.pallas_call(
        functools.partial(_sgu_kernel, tm=tm),
        grid=(n_rows // tm,),
        in_specs=[pl.BlockSpec((tm, MIX_W), lambda i: (i, COL_U)),
                  pl.BlockSpec((tm, MIX_W), lambda i: (i, COL_V)),
                  pl.BlockSpec((N_GROUPS, CHUNK, CHUNK), lambda i: (0, 0, 0)),
                  pl.BlockSpec((CHUNK, MIX_W), lambda i: (0, 0))],
        out_specs=pl.BlockSpec((tm, MIX_W), lambda i: (i, 0)),
        out_shape=jax.ShapeDtypeStruct((rows.m, MIX_W), bf16),
        compiler_params=_cparams("parallel"),
        name="sgu",
    )(proj, proj, sgu_w, sgu_bias)


def _conv_kernel(b_ref, c_ref, x_ref, w_ref, o_ref, *, tm, n_x_tiles, ctx_len):
    y = c_ref[...].astype(f32) * x_ref[...].astype(f32)
    r = lax.broadcasted_iota(jnp.int32, y.shape, 0)
    period = jnp.where(pl.program_id(0) < n_x_tiles, GRID_W, ctx_len)
    rp = r & (period - 1)
    prev = jnp.where(rp == 0, 0.0, pltpu.roll(y, 1, axis=0))
    nxt = jnp.where(rp == period - 1, 0.0, pltpu.roll(y, tm - 1, axis=0))
    conv = w_ref[0:1, :] * prev + w_ref[1:2, :] * y + w_ref[2:3, :] * nxt
    o_ref[...] = (b_ref[...].astype(f32) * conv).astype(bf16)


def _conv(proj, conv_w, rows, tm, n_rows):
    assert GRID_W & (GRID_W - 1) == 0 and rows.ctx_len & (rows.ctx_len - 1) == 0
    assert tm % GRID_W == 0 and tm % rows.ctx_len == 0
    return pl.pallas_call(
        functools.partial(_conv_kernel, tm=tm, n_x_tiles=rows.m_x // tm, ctx_len=rows.ctx_len),
        grid=(n_rows // tm,),
        in_specs=[pl.BlockSpec((tm, MIX_W), lambda i: (i, COL_SB)),
                  pl.BlockSpec((tm, MIX_W), lambda i: (i, COL_SC)),
                  pl.BlockSpec((tm, MIX_W), lambda i: (i, COL_SX)),
                  pl.BlockSpec((3, MIX_W), lambda i: (0, 0))],
        out_specs=pl.BlockSpec((tm, MIX_W), lambda i: (i, 0)),
        out_shape=jax.ShapeDtypeStruct((rows.m, MIX_W), bf16),
        compiler_params=_cparams("parallel"),
        name="conv",
    )(proj, proj, proj, conv_w)


def _merge_kernel(hx_ref, ya_ref, yb_ref, yc_ref, yd_ref, wg_ref, wb_ref, o_ref, acc, *, row_chunk):
    n = pl.program_id(2)
    for k, y_ref in enumerate((ya_ref, yb_ref, yc_ref, yd_ref)):
        @pl.when(n == k)
        def _(k=k, y_ref=y_ref):
            for r in range(hx_ref.shape[0] // row_chunk):
                rows = slice(r * row_chunk, (r + 1) * row_chunk)
                gate = jax.nn.sigmoid(jnp.dot(hx_ref[rows, :], wg_ref[...], preferred_element_type=f32))
                term = gate * jnp.dot(y_ref[rows, :], wb_ref[...], preferred_element_type=f32)
                if k == 0:
                    acc[rows, :] = term
                elif k < N_BRANCH - 1:
                    acc[rows, :] += term
                else:
                    o_ref[rows, :] = (acc[rows, :] + term).astype(bf16)


def _merge(hx, ys, w_gate, w_branch, tm, tn, n_rows):
    d = hx.shape[1]
    m = n_rows
    nj = d // tn
    y_spec = pl.BlockSpec((tm, MIX_W), lambda i, j, n: (i, 0))
    return pl.pallas_call(
        functools.partial(_merge_kernel, row_chunk=_tile(tm, MERGE_ROW_CHUNK)),
        grid=(m // tm, nj, N_BRANCH),
        in_specs=[pl.BlockSpec((tm, d), lambda i, j, n: (i, 0)), y_spec, y_spec, y_spec, y_spec,
                  pl.BlockSpec((d, tn), lambda i, j, n: (0, n * nj + j)),
                  pl.BlockSpec((None, MIX_W, tn), lambda i, j, n: (n, 0, j))],
        out_specs=pl.BlockSpec((tm, tn), lambda i, j, n: (i, j)),
        out_shape=jax.ShapeDtypeStruct((m, d), bf16),
        scratch_shapes=[pltpu.VMEM((tm, tn), f32)],
        compiler_params=_cparams("parallel", "arbitrary", "arbitrary"),
        name="merge",
    )(hx, *ys, w_gate, w_branch)


def _outproj_kernel(m_ref, w_ref, x_ref, g1_ref, g2_ref, gate_ref, sh_ref, sc_ref, xo_ref, ho_ref, *, row_chunk):
    for r in range(m_ref.shape[0] // row_chunk):
        rows = slice(r * row_chunk, (r + 1) * row_chunk)
        mix = jnp.dot(m_ref[rows, :], w_ref[...], preferred_element_type=f32)
        xn = x_ref[rows, :] + gate_ref[...] * (_rms(mix) * g1_ref[...])
        xo_ref[rows, :] = xn
        ho_ref[rows, :] = (_rms(xn) * g2_ref[...] * (1.0 + sc_ref[...]) + sh_ref[...]).astype(bf16)


def _outproj(merged, w_out, xs, g1, g2, mods, layer, rows, n_rows):
    d = xs.shape[1]
    tm = rows.tm
    row = pl.BlockSpec((tm, d), lambda i: (i, 0))
    return pl.pallas_call(
        functools.partial(_outproj_kernel, row_chunk=_tile(tm, OUTPROJ_ROW_CHUNK)),
        grid=(n_rows // tm,),
        in_specs=[row, pl.BlockSpec((d, d), lambda i: (0, 0)), row, _vec_spec(d), _vec_spec(d),
                  rows.mod_spec(layer, 2, d), rows.mod_spec(layer, 3, d), rows.mod_spec(layer, 4, d)],
        out_specs=[row, row],
        out_shape=[jax.ShapeDtypeStruct((n_rows, d), f32), jax.ShapeDtypeStruct((n_rows, d), bf16)],
        compiler_params=_cparams("parallel"),
        name="outproj",
    )(merged, w_out, xs, g1, g2, mods, mods, mods)


def _ffn_kernel(*refs, n_f, has_next, row_chunk):
    if has_next:
        h_ref, wa_ref, wb_ref, w2_ref, x_ref, g3_ref, gate_ref, gn_ref, sh_ref, sc_ref, xo_ref, ho_ref = refs
    else:
        h_ref, wa_ref, wb_ref, w2_ref, x_ref, g3_ref, gate_ref, xo_ref = refs
    f = pl.program_id(1)

    @pl.when(f == 0)
    def _():
        xo_ref[...] = jnp.zeros_like(xo_ref)

    for r in range(h_ref.shape[0] // row_chunk):
        rows = slice(r * row_chunk, (r + 1) * row_chunk)
        h = h_ref[rows, :]
        a = jnp.dot(h, wa_ref[...], preferred_element_type=f32)
        b = jnp.dot(h, wb_ref[...], preferred_element_type=f32)
        xo_ref[rows, :] += jnp.dot((jax.nn.silu(a) * b).astype(bf16), w2_ref[...], preferred_element_type=f32)

    @pl.when(f == n_f - 1)
    def _():
        xn = x_ref[...] + gate_ref[...] * (_rms(xo_ref[...]) * g3_ref[...])
        xo_ref[...] = xn
        if has_next:
            ho_ref[...] = (_rms(xn) * gn_ref[...] * (1.0 + sc_ref[...]) + sh_ref[...]).astype(bf16)


def _ffn(hx, w1, w2, xs, g3, g_next, mods, layer, rows, tf, n_rows):
    d = xs.shape[1]
    d_ff = w2.shape[0]
    n_f = d_ff // tf
    tm = rows.tm
    has_next = g_next is not None
    row = pl.BlockSpec((tm, d), lambda i, f: (i, 0))
    in_specs = [row,
                pl.BlockSpec((d, tf), lambda i, f: (0, f)),
                pl.BlockSpec((d, tf), lambda i, f: (0, n_f + f)),
                pl.BlockSpec((tf, d), lambda i, f: (f, 0)),
                row, _vec_spec(d), rows.mod_spec(layer, 5, d)]
    args = [hx, w1, w1, w2, xs, g3, mods]
    out_specs = [row]
    out_shape = [jax.ShapeDtypeStruct((n_rows, d), f32)]
    if has_next:
        in_specs += [_vec_spec(d), rows.mod_spec(layer + 1, 0, d), rows.mod_spec(layer + 1, 1, d)]
        args += [g_next, mods, mods]
        out_specs.append(row)
        out_shape.append(jax.ShapeDtypeStruct((n_rows, d), bf16))
    out = pl.pallas_call(
        functools.partial(_ffn_kernel, n_f=n_f, has_next=has_next, row_chunk=_tile(tm, FFN_ROW_CHUNK)),
        grid=(n_rows // tm, n_f),
        in_specs=in_specs,
        out_specs=out_specs,
        out_shape=out_shape,
        compiler_params=_cparams("parallel", "arbitrary"),
        name="ffn",
    )(*args)
    return out if has_next else (out[0], None)


def kernel(x, c, ctx, c_ctx, ada_w, ada_b, norm_g, w_in, ret_log_decay, conv_w, sgu_w, sgu_b,
           w_branch, w_out, ffn_w_in, ffn_w_out):
    batch, seq, d = x.shape
    ctx_len = ctx.shape[1]
    depth = ada_w.shape[0]
    d_ff = ffn_w_out.shape[1]
    assert batch + 1 <= MOD_ROWS and w_in.shape[2] == N_MIX_COLS + N_BRANCH * d

    rows = _Rows(batch, seq, ctx_len, _tile(batch * ctx_len, 512))
    tm_big = _tile(batch * ctx_len, 1024)

    cc = jnp.zeros((MOD_ROWS, d), f32).at[:batch].set(c).at[batch].set(c_ctx)
    mods = _adaln(cc, ada_w, ada_b).reshape(depth * MOD_ROWS * 6, 1, d)

    rope_tabs = _rope_tables(seq)
    dft_x = _dft_tables(seq)
    dft_c = _dft_tables(ctx_len)

    xs, hx = _norm_mod(x.reshape(batch * seq, d), ctx.reshape(batch * ctx_len, d),
                       norm_g[0, 0:1], mods, rows)

    for l in range(depth):
        w_mix = w_in[l, :, :N_MIX_COLS].astype(bf16)
        w_gate = w_in[l, :, N_MIX_COLS:].astype(bf16)
        sgu_bias = jnp.repeat(jnp.transpose(sgu_b[l]), GROUP_DIM, axis=1)

        last = l + 1 == depth
        n_rows = rows.m_x if last else rows.m

        proj = _matmul(hx, w_mix, tm_big, _tile(N_MIX_COLS, 1024), "inproj")
        ya = _retention(proj, ret_log_decay[l], rows, rope_tabs, None)
        yb = _fourier(proj, rows, dft_x, None)
        if not last:
            ya = _retention(proj, ret_log_decay[l], rows, None, ya)
            yb = _fourier(proj, rows, dft_c, yb)
        yc = _sgu(proj, sgu_w[l].astype(bf16), sgu_bias, rows, rows.tm, n_rows)
        yd = _conv(proj, conv_w[l], rows, rows.tm, n_rows)
        merged = _merge(hx, (ya, yb, yc, yd), w_gate, w_branch[l].astype(bf16), tm_big, _tile(d, 1024), n_rows)
        xs, hx2 = _outproj(merged, w_out[l].astype(bf16), xs, norm_g[l, 1:2], norm_g[l, 2:3], mods, l, rows,
                           n_rows)
        g_next = None if last else norm_g[l + 1, 0:1]
        xs, hx = _ffn(hx2, ffn_w_in[l].astype(bf16), ffn_w_out[l].astype(bf16), xs, norm_g[l, 3:4],
                      g_next, mods, l, rows, _tile(d_ff, 512), n_rows)

    return xs.reshape(batch, seq, d)
```

```python
import functools

import jax
import jax.numpy as jnp
from jax import lax
from jax.experimental import pallas as pl
from jax.experimental.pallas import tpu as pltpu

f32 = jnp.float32
bf16 = jnp.bfloat16

GRID_W = 64
MIX_W = 512
N_BRANCH = 4
GROUP_DIM = 128
N_GROUPS = MIX_W // GROUP_DIM
CHUNK = 128
ROPE_BASE = 10000.0
EPS = 1e-6
N_MIX_COLS = 10 * MIX_W
COL_F, COL_U, COL_V, COL_SB, COL_SC, COL_SX = 4, 5, 6, 7, 8, 9
MOD_ROWS = 16
CHUNK_UNROLL = 4
FFN_ROW_CHUNK = 256
MERGE_ROW_CHUNK = 512
OUTPROJ_ROW_CHUNK = 256

VMEM_LIMIT_BYTES = 56 * 1024 * 1024


def _cparams(*sem):
    return pltpu.CompilerParams(dimension_semantics=sem, vmem_limit_bytes=VMEM_LIMIT_BYTES)


def _tile(n, pref):
    t = min(n, pref)
    while n % t:
        t //= 2
    return t


def _rms(x):
    return x * lax.rsqrt(jnp.mean(x * x, axis=-1, keepdims=True) + EPS)


def _adaln_kernel(c_ref, w_ref, b_ref, o_ref):
    s = jax.nn.silu(c_ref[...]).astype(bf16)
    o_ref[...] = jnp.dot(s, w_ref[...].astype(bf16), preferred_element_type=f32) + b_ref[...]


def _adaln(cc, ada_w, ada_b):
    depth, d, n6 = ada_w.shape
    tn = _tile(n6, 1024)
    return pl.pallas_call(
        _adaln_kernel,
        grid=(depth, n6 // tn),
        in_specs=[
            pl.BlockSpec((MOD_ROWS, d), lambda l, j: (0, 0)),
            pl.BlockSpec((None, d, tn), lambda l, j: (l, 0, j)),
            pl.BlockSpec((None, 1, tn), lambda l, j: (l, 0, j)),
        ],
        out_specs=pl.BlockSpec((None, MOD_ROWS, tn), lambda l, j: (l, 0, j)),
        out_shape=jax.ShapeDtypeStruct((depth, MOD_ROWS, n6), f32),
        compiler_params=_cparams("parallel", "parallel"),
        name="adaln",
    )(cc, ada_w, ada_b.reshape(depth, 1, n6))


class _Rows:
    def __init__(self, batch, seq, ctx_len, tm):
        self.batch, self.seq, self.ctx_len, self.tm = batch, seq, ctx_len, tm
        self.m_x, self.m_c = batch * seq, batch * ctx_len
        self.m = self.m_x + self.m_c
        assert seq % tm == 0 and self.m_c % tm == 0
        self.n_x_tiles = self.m_x // tm
        self.n_tiles = self.m // tm
        self.per_batch = seq // tm

    def mod_spec(self, layer, k, d, extra_axes=0):
        def index(i, *_):
            r = jnp.where(i < self.n_x_tiles, i // self.per_batch, self.batch)
            return ((layer * MOD_ROWS + r) * 6 + k, 0, 0)
        return pl.BlockSpec((None, 1, d), index)


def _vec_spec(d):
    return pl.BlockSpec((1, d), lambda i, *_: (0, 0))


def _norm_mod_kernel(x_ref, c_ref, g_ref, sh_ref, sc_ref, xs_ref, hx_ref, *, n_x_tiles):
    def emit(x):
        xs_ref[...] = x
        hx_ref[...] = (_rms(x) * g_ref[...] * (1.0 + sc_ref[...]) + sh_ref[...]).astype(bf16)

    @pl.when(pl.program_id(0) < n_x_tiles)
    def _():
        emit(x_ref[...])

    @pl.when(pl.program_id(0) >= n_x_tiles)
    def _():
        emit(c_ref[...])


def _norm_mod(x2d, c2d, g, mods, rows):
    d = x2d.shape[1]
    tm, nxt = rows.tm, rows.n_x_tiles
    return pl.pallas_call(
        functools.partial(_norm_mod_kernel, n_x_tiles=nxt),
        grid=(rows.n_tiles,),
        in_specs=[
            pl.BlockSpec((tm, d), lambda i: (jnp.minimum(i, nxt - 1), 0)),
            pl.BlockSpec((tm, d), lambda i: (jnp.maximum(i - nxt, 0), 0)),
            _vec_spec(d),
            rows.mod_spec(0, 0, d),
            rows.mod_spec(0, 1, d),
        ],
        out_specs=[pl.BlockSpec((tm, d), lambda i: (i, 0)), pl.BlockSpec((tm, d), lambda i: (i, 0))],
        out_shape=[jax.ShapeDtypeStruct((rows.m, d), f32), jax.ShapeDtypeStruct((rows.m, d), bf16)],
        compiler_params=_cparams("parallel"),
        name="norm_mod",
    )(x2d, c2d, g, mods, mods)


def _matmul_kernel(a_ref, b_ref, o_ref):
    o_ref[...] = jnp.dot(a_ref[...], b_ref[...], preferred_element_type=f32).astype(o_ref.dtype)


def _col_blocks(w, tn):
    k, n = w.shape
    return jnp.transpose(w.reshape(k, n // tn, tn), (1, 0, 2))


def _matmul(a, b_blocks, tm, name):
    m, k = a.shape
    nj, _, tn = b_blocks.shape
    return pl.pallas_call(
        _matmul_kernel,
        grid=(m // tm, nj),
        in_specs=[pl.BlockSpec((tm, k), lambda i, j: (i, 0)), pl.BlockSpec((None, k, tn), lambda i, j: (j, 0, 0))],
        out_specs=pl.BlockSpec((tm, tn), lambda i, j: (i, j)),
        out_shape=jax.ShapeDtypeStruct((m, nj * tn), bf16),
        compiler_params=_cparams("parallel", "arbitrary"),
        name=name,
    )(a, b_blocks)


def _rope_tables(seq):
    t = jnp.arange(seq)
    n = GROUP_DIM // 2
    freqs = ROPE_BASE ** (-jnp.arange(0, n, 2, dtype=f32) / n)

    def tab(pos):
        ang = pos.astype(f32)[:, None] * freqs[None, :]
        c, s = jnp.cos(ang), jnp.sin(ang)
        return jnp.concatenate([c, c], -1), jnp.concatenate([-s, s], -1)

    c1, s1 = tab(t // GRID_W)
    c2, s2 = tab(t % GRID_W)
    dst = jnp.arange(GROUP_DIM)
    src = jnp.where((dst % n) < n // 2, dst + n // 2, dst - n // 2)
    perm = (jnp.arange(GROUP_DIM)[:, None] == src[None, :]).astype(bf16)
    return jnp.concatenate([c1, c2], -1), jnp.concatenate([s1, s2], -1), perm


def _retention_kernel(*refs, n_chunks, rope, ctx_len):
    refs = list(refs)
    lg_ref, q_ref, k_ref, v_ref, g_ref = refs[:5]
    pos = 5
    if ctx_len:
        kc_ref, vc_ref = refs[pos:pos + 2]
        pos += 2
    if rope:
        cos_ref, sin_ref, perm_ref = refs[pos:pos + 3]
        pos += 3
    o_ref, qs, ks, sf, sb = refs[pos:]

    c_len, dh = CHUNK, GROUP_DIM
    h = pl.program_id(1)
    lgf, lgb = lg_ref[0, h], lg_ref[1, h]
    scale = dh ** -0.5
    ri = lax.broadcasted_iota(jnp.int32, (c_len, dh), 0).astype(f32)
    ci = lax.broadcasted_iota(jnp.int32, (c_len, dh), 1).astype(f32)
    rel = ri - ci
    decay = (jnp.where(rel >= 0, jnp.exp(lgf * jnp.maximum(rel, 0.0)), 0.0)
             + jnp.where(rel <= 0, jnp.exp(lgb * jnp.maximum(-rel, 0.0)), 0.0))
    kw_f = jnp.exp(lgf * (c_len - 1 - ri))
    kw_b = jnp.exp(lgb * ri)
    qw_f = jnp.exp(lgf * (ri + 1.0))
    qw_b = jnp.exp(lgb * (c_len - ri))
    cd_f = jnp.exp(jnp.full((dh, dh), lgf * c_len, f32))
    cd_b = jnp.exp(jnp.full((dh, dh), lgb * c_len, f32))

    def rows_of(c):
        return pl.ds(pl.multiple_of(c * c_len, c_len), c_len)

    def kt_dot(kb, vw):
        return lax.dot_general(kb, vw, (((0,), (0,)), ((), ())), preferred_element_type=f32)

    def prep(c, carry):
        rows = rows_of(c)
        q, k = q_ref[rows, :], k_ref[rows, :]
        qf, kf = q.astype(f32), k.astype(f32)
        if rope:
            cos, sin = cos_ref[rows, :], sin_ref[rows, :]
            qf = qf * cos + jnp.dot(q, perm_ref[...], preferred_element_type=f32) * sin
            kf = kf * cos + jnp.dot(k, perm_ref[...], preferred_element_type=f32) * sin
        kb = (kf * scale).astype(bf16)
        qs[rows, :] = qf.astype(bf16)
        ks[rows, :] = kb
        v = v_ref[rows, :].astype(f32)
        vw = jnp.concatenate([v * kw_f, v * kw_b], axis=1).astype(bf16)
        ds = kt_dot(kb, vw)
        sf[c] = ds[:, :dh]
        sb[c] = ds[:, dh:]
        return carry

    unroll = min(n_chunks, CHUNK_UNROLL)
    lax.fori_loop(0, n_chunks, prep, 0, unroll=unroll)

    if ctx_len:
        mi = lax.broadcasted_iota(jnp.int32, (ctx_len, dh), 0).astype(f32)
        kcb = (kc_ref[...].astype(f32) * scale).astype(bf16)
        vc = vc_ref[...].astype(f32)
        vcw = jnp.concatenate([vc * jnp.exp(lgf * (ctx_len - 1 - mi)), vc * jnp.exp(lgb * mi)],
                              axis=1).astype(bf16)
        s0 = kt_dot(kcb, vcw)
        s0_f, s0_b = s0[:, :dh], s0[:, dh:]
    else:
        s0_f = s0_b = jnp.zeros((dh, dh), f32)

    def scan_f(c, s):
        nxt = cd_f * s + sf[c]
        sf[c] = s
        return nxt

    def scan_b(j, s):
        c = n_chunks - 1 - j
        nxt = cd_b * s + sb[c]
        sb[c] = s
        return nxt

    lax.fori_loop(0, n_chunks, scan_f, s0_f)
    lax.fori_loop(0, n_chunks, scan_b, s0_b)

    def emit(c, carry):
        rows = rows_of(c)
        q, k, v = qs[rows, :], ks[rows, :], v_ref[rows, :]
        sc = lax.dot_general(q, k, (((1,), (1,)), ((), ())), preferred_element_type=f32) * decay
        inner = jnp.dot(sc.astype(bf16), v, preferred_element_type=f32)
        s2 = jnp.concatenate([sf[c], sb[c]], axis=1).astype(bf16)
        cr = jnp.dot(q, s2, preferred_element_type=f32)
        ret = inner + qw_f * cr[:, :dh] + qw_b * cr[:, dh:]
        g = g_ref[rows, :].astype(f32)
        o_ref[rows, :] = (jax.nn.silu(g) * _rms(ret)).astype(bf16)
        return carry

    lax.fori_loop(0, n_chunks, emit, 0, unroll=unroll)


def _retention(proj, log_decay, rows, rope_tabs, prev_out):
    latent = prev_out is None
    seq = rows.seq if latent else rows.ctx_len
    n_chunks = seq // CHUNK
    row0 = 0 if latent else rows.m_x // seq
    hq = MIX_W // GROUP_DIM

    def col(off):
        return pl.BlockSpec((seq, GROUP_DIM), lambda b, h: (row0 + b, off * hq + h))

    in_specs = [pl.BlockSpec(memory_space=pltpu.SMEM), col(0), col(1), col(2), col(3)]
    args = [log_decay, proj, proj, proj, proj]
    if latent:
        cb0 = rows.m_x // rows.ctx_len
        for off in (1, 2):
            in_specs.append(pl.BlockSpec((rows.ctx_len, GROUP_DIM),
                                         lambda b, h, off=off: (cb0 + b, off * hq + h)))
            args.append(proj)
        cos, sin, perm = rope_tabs
        in_specs += [pl.BlockSpec((seq, GROUP_DIM), lambda b, h: (0, 0)),
                     pl.BlockSpec((seq, GROUP_DIM), lambda b, h: (0, 0)),
                     pl.BlockSpec((GROUP_DIM, GROUP_DIM), lambda b, h: (0, 0))]
        args += [cos, sin, perm]
        aliases = {}
    else:
        in_specs.append(pl.BlockSpec(memory_space=pl.ANY))
        args.append(prev_out)
        aliases = {len(args) - 1: 0}

    kernel = functools.partial(_retention_kernel, n_chunks=n_chunks, rope=latent,
                               ctx_len=rows.ctx_len if latent else 0)
    if not latent:
        def kernel(*refs, _k=kernel):
            return _k(*refs[:5], *refs[6:])

    return pl.pallas_call(
        kernel,
        grid=(rows.batch, hq),
        in_specs=in_specs,
        out_specs=pl.BlockSpec((seq, GROUP_DIM), lambda b, h: (row0 + b, h)),
        out_shape=jax.ShapeDtypeStruct((rows.m, MIX_W), bf16),
        scratch_shapes=[pltpu.VMEM((seq, GROUP_DIM), bf16), pltpu.VMEM((seq, GROUP_DIM), bf16),
                        pltpu.VMEM((n_chunks, GROUP_DIM, GROUP_DIM), f32),
                        pltpu.VMEM((n_chunks, GROUP_DIM, GROUP_DIM), f32)],
        input_output_aliases=aliases,
        compiler_params=_cparams("parallel", "parallel"),
        name="retention_x" if latent else "retention_ctx",
    )(*args)


def _dft_tables(seq):
    def cs(n):
        i = jnp.arange(n, dtype=jnp.int32)
        ang = ((i[:, None] * i[None, :]) % n).astype(f32) * (2.0 * jnp.pi / n)
        return jnp.cos(ang), jnp.sin(ang)

    cc, sc = cs(GROUP_DIM)
    cl, sl = cs(seq)
    return jnp.concatenate([cc, sc], 1).astype(bf16), jnp.concatenate([cl, -sl], 1).astype(bf16)


def _fourier_kernel(z_ref, ch_ref, pos_ref, o_ref, zz, *, seq):
    @pl.when(pl.program_id(1) == 0)
    def _():
        norm = (seq * GROUP_DIM) ** -0.5
        for g in range(N_GROUPS):
            cols = slice(g * GROUP_DIM, (g + 1) * GROUP_DIM)
            t = jnp.dot(z_ref[:, cols], ch_ref[...], preferred_element_type=f32) * norm
            zz[0:seq, cols] = t[:, :GROUP_DIM].astype(bf16)
            zz[seq:2 * seq, cols] = t[:, GROUP_DIM:].astype(bf16)

    o_ref[...] = jnp.dot(pos_ref[...], zz[...], preferred_element_type=f32).astype(bf16)


def _fourier(proj, rows, tables, prev_out):
    latent = prev_out is None
    seq = rows.seq if latent else rows.ctx_len
    ch_tab, pos_tab = tables
    tm = _tile(seq, 512)
    n_mt = seq // tm
    row0 = 0 if latent else rows.m_x // seq
    in_specs = [pl.BlockSpec((seq, MIX_W), lambda b, m: (row0 + b, COL_F)),
                pl.BlockSpec((GROUP_DIM, 2 * GROUP_DIM), lambda b, m: (0, 0)),
                pl.BlockSpec((tm, 2 * seq), lambda b, m: (m, 0))]
    args = [proj, ch_tab, pos_tab]
    kernel = functools.partial(_fourier_kernel, seq=seq)
    aliases = {}
    if not latent:
        in_specs.append(pl.BlockSpec(memory_space=pl.ANY))
        args.append(prev_out)
        aliases = {3: 0}

        def kernel(z_ref, ch_ref, pos_ref, _carrier, o_ref, zz, _k=kernel):
            return _k(z_ref, ch_ref, pos_ref, o_ref, zz)

    return pl.pallas_call(
        kernel,
        grid=(rows.batch, n_mt),
        in_specs=in_specs,
        out_specs=pl.BlockSpec((tm, MIX_W), lambda b, m: ((row0 + b) * n_mt + m, 0)),
        out_shape=jax.ShapeDtypeStruct((rows.m, MIX_W), bf16),
        scratch_shapes=[pltpu.VMEM((2 * seq, MIX_W), bf16)],
        input_output_aliases=aliases,
        compiler_params=_cparams("parallel", "arbitrary"),
        name="fourier_x" if latent else "fourier_ctx",
    )(*args)


def _sgu_kernel(u_ref, v_ref, w_ref, b_ref, o_ref, *, tm):
    for ch in range(tm // CHUNK):
        rows = slice(ch * CHUNK, (ch + 1) * CHUNK)
        u = jax.nn.gelu(u_ref[rows, :].astype(f32))
        v = jax.nn.gelu(v_ref[rows, :].astype(f32))
        outs = []
        for g in range(N_GROUPS):
            cols = slice(g * GROUP_DIM, (g + 1) * GROUP_DIM)
            vg = v[:, cols]
            dv = vg - jnp.mean(vg, axis=-1, keepdims=True)
            vn = dv * lax.rsqrt(jnp.mean(dv * dv, axis=-1, keepdims=True) + EPS)
            s = jnp.dot(w_ref[g], vn.astype(bf16), preferred_element_type=f32) + b_ref[:, cols]
            outs.append(u[:, cols] * s)
        o_ref[rows, :] = jnp.concatenate(outs, axis=1).astype(bf16)


def _sgu(proj, sgu_w, sgu_bias, rows, tm, n_rows):
    return pl.pallas_call(
        functools.partial(_sgu_kernel, tm=tm),
        grid=(n_rows // tm,),
        in_specs=[pl.BlockSpec((tm, MIX_W), lambda i: (i, COL_U)),
                  pl.BlockSpec((tm, MIX_W), lambda i: (i, COL_V)),
                  pl.BlockSpec((N_GROUPS, CHUNK, CHUNK), lambda i: (0, 0, 0)),
                  pl.BlockSpec((CHUNK, MIX_W), lambda i: (0, 0))],
        out_specs=pl.BlockSpec((tm, MIX_W), lambda i: (i, 0)),
        out_shape=jax.ShapeDtypeStruct((rows.m, MIX_W), bf16),
        compiler_params=_cparams("parallel"),
        name="sgu",
    )(proj, proj, sgu_w, sgu_bias)


def _conv_kernel(b_ref, c_ref, x_ref, w_ref, o_ref, *, tm, n_x_tiles, ctx_len):
    y = c_ref[...].astype(f32) * x_ref[...].astype(f32)
    r = lax.broadcasted_iota(jnp.int32, y.shape, 0)
    period = jnp.where(pl.program_id(0) < n_x_tiles, GRID_W, ctx_len)
    rp = r & (period - 1)
    prev = jnp.where(rp == 0, 0.0, pltpu.roll(y, 1, axis=0))
    nxt = jnp.where(rp == period - 1, 0.0, pltpu.roll(y, tm - 1, axis=0))
    conv = w_ref[0:1, :] * prev + w_ref[1:2, :] * y + w_ref[2:3, :] * nxt
    o_ref[...] = (b_ref[...].astype(f32) * conv).astype(bf16)


def _conv(proj, conv_w, rows, tm, n_rows):
    assert GRID_W & (GRID_W - 1) == 0 and rows.ctx_len & (rows.ctx_len - 1) == 0
    assert tm % GRID_W == 0 and tm % rows.ctx_len == 0
    return pl.pallas_call(
        functools.partial(_conv_kernel, tm=tm, n_x_tiles=rows.m_x // tm, ctx_len=rows.ctx_len),
        grid=(n_rows // tm,),
        in_specs=[pl.BlockSpec((tm, MIX_W), lambda i: (i, COL_SB)),
                  pl.BlockSpec((tm, MIX_W), lambda i: (i, COL_SC)),
                  pl.BlockSpec((tm, MIX_W), lambda i: (i, COL_SX)),
                  pl.BlockSpec((3, MIX_W), lambda i: (0, 0))],
        out_specs=pl.BlockSpec((tm, MIX_W), lambda i: (i, 0)),
        out_shape=jax.ShapeDtypeStruct((rows.m, MIX_W), bf16),
        compiler_params=_cparams("parallel"),
        name="conv",
    )(proj, proj, proj, conv_w)


def _merge_kernel(hx_ref, ya_ref, yb_ref, yc_ref, yd_ref, wg_ref, wb_ref, o_ref, acc, *, row_chunk):
    n = pl.program_id(2)
    for k, y_ref in enumerate((ya_ref, yb_ref, yc_ref, yd_ref)):
        @pl.when(n == k)
        def _(k=k, y_ref=y_ref):
            for r in range(hx_ref.shape[0] // row_chunk):
                rows = slice(r * row_chunk, (r + 1) * row_chunk)
                gate = jax.nn.sigmoid(jnp.dot(hx_ref[rows, :], wg_ref[...], preferred_element_type=f32))
                term = gate * jnp.dot(y_ref[rows, :], wb_ref[...], preferred_element_type=f32)
                if k == 0:
                    acc[rows, :] = term
                elif k < N_BRANCH - 1:
                    acc[rows, :] += term
                else:
                    o_ref[rows, :] = (acc[rows, :] + term).astype(bf16)


def _merge(hx, ys, w_gate, w_branch, tm, n_rows):
    d = hx.shape[1]
    m = n_rows
    _, nj, _, tn = w_branch.shape
    y_spec = pl.BlockSpec((tm, MIX_W), lambda i, j, n: (i, 0))
    return pl.pallas_call(
        functools.partial(_merge_kernel, row_chunk=_tile(tm, MERGE_ROW_CHUNK)),
        grid=(m // tm, nj, N_BRANCH),
        in_specs=[pl.BlockSpec((tm, d), lambda i, j, n: (i, 0)), y_spec, y_spec, y_spec, y_spec,
                  pl.BlockSpec((None, d, tn), lambda i, j, n: (n * nj + j, 0, 0)),
                  pl.BlockSpec((None, None, MIX_W, tn), lambda i, j, n: (n, j, 0, 0))],
        out_specs=pl.BlockSpec((tm, tn), lambda i, j, n: (i, j)),
        out_shape=jax.ShapeDtypeStruct((m, d), bf16),
        scratch_shapes=[pltpu.VMEM((tm, tn), f32)],
        compiler_params=_cparams("parallel", "arbitrary", "arbitrary"),
        name="merge",
    )(hx, *ys, w_gate, w_branch)


def _outproj_kernel(m_ref, w_ref, x_ref, g1_ref, g2_ref, gate_ref, sh_ref, sc_ref, xo_ref, ho_ref, *, row_chunk):
    res_scale = gate_ref[...] * g1_ref[...]
    mod_scale = g2_ref[...] * (1.0 + sc_ref[...])
    for r in range(m_ref.shape[0] // row_chunk):
        rows = slice(r * row_chunk, (r + 1) * row_chunk)
        mix = jnp.dot(m_ref[rows, :], w_ref[...], preferred_element_type=f32)
        xn = x_ref[rows, :] + _rms(mix) * res_scale
        xo_ref[rows, :] = xn
        ho_ref[rows, :] = (_rms(xn) * mod_scale + sh_ref[...]).astype(bf16)


def _outproj(merged, w_out, xs, g1, g2, mods, layer, rows, n_rows):
    d = xs.shape[1]
    tm = rows.tm
    row = pl.BlockSpec((tm, d), lambda i: (i, 0))
    return pl.pallas_call(
        functools.partial(_outproj_kernel, row_chunk=_tile(tm, OUTPROJ_ROW_CHUNK)),
        grid=(n_rows // tm,),
        in_specs=[row, pl.BlockSpec((d, d), lambda i: (0, 0)), row, _vec_spec(d), _vec_spec(d),
                  rows.mod_spec(layer, 2, d), rows.mod_spec(layer, 3, d), rows.mod_spec(layer, 4, d)],
        out_specs=[row, row],
        out_shape=[jax.ShapeDtypeStruct((n_rows, d), f32), jax.ShapeDtypeStruct((n_rows, d), bf16)],
        compiler_params=_cparams("parallel"),
        name="outproj",
    )(merged, w_out, xs, g1, g2, mods, mods, mods)


def _ffn_kernel(*refs, n_f, has_next, row_chunk):
    if has_next:
        h_ref, wa_ref, wb_ref, w2_ref, x_ref, g3_ref, gate_ref, gn_ref, sh_ref, sc_ref, xo_ref, ho_ref = refs
    else:
        h_ref, wa_ref, wb_ref, w2_ref, x_ref, g3_ref, gate_ref, xo_ref = refs
    f = pl.program_id(1)

    def step(first, last):
        if last:
            res_scale = gate_ref[...] * g3_ref[...]
            if has_next:
                mod_scale = gn_ref[...] * (1.0 + sc_ref[...])
        for r in range(h_ref.shape[0] // row_chunk):
            rows = slice(r * row_chunk, (r + 1) * row_chunk)
            h = h_ref[rows, :]
            a = jnp.dot(h, wa_ref[...], preferred_element_type=f32)
            b = jnp.dot(h, wb_ref[...], preferred_element_type=f32)
            acc = jnp.dot((jax.nn.silu(a) * b).astype(bf16), w2_ref[...], preferred_element_type=f32)
            if not first:
                acc = xo_ref[rows, :] + acc
            if last:
                acc = x_ref[rows, :] + _rms(acc) * res_scale
                if has_next:
                    ho_ref[rows, :] = (_rms(acc) * mod_scale + sh_ref[...]).astype(bf16)
            xo_ref[rows, :] = acc

    if n_f == 1:
        step(True, True)
    else:
        pl.when(f == 0)(lambda: step(True, False))
        pl.when(f == n_f - 1)(lambda: step(False, True))
        if n_f > 2:
            pl.when((f > 0) & (f < n_f - 1))(lambda: step(False, False))


def _ffn(hx, w1_blocks, w2, xs, g3, g_next, mods, layer, rows, n_rows):
    d = xs.shape[1]
    n_f, tf = w1_blocks.shape[0] // 2, w1_blocks.shape[2]
    tm = rows.tm
    has_next = g_next is not None
    row = pl.BlockSpec((tm, d), lambda i, f: (i, 0))
    in_specs = [row,
                pl.BlockSpec((None, d, tf), lambda i, f: (f, 0, 0)),
                pl.BlockSpec((None, d, tf), lambda i, f: (n_f + f, 0, 0)),
                pl.BlockSpec((tf, d), lambda i, f: (f, 0)),
                row, _vec_spec(d), rows.mod_spec(layer, 5, d)]
    args = [hx, w1_blocks, w1_blocks, w2, xs, g3, mods]
    out_specs = [row]
    out_shape = [jax.ShapeDtypeStruct((n_rows, d), f32)]
    if has_next:
        in_specs += [_vec_spec(d), rows.mod_spec(layer + 1, 0, d), rows.mod_spec(layer + 1, 1, d)]
        args += [g_next, mods, mods]
        out_specs.append(row)
        out_shape.append(jax.ShapeDtypeStruct((n_rows, d), bf16))
    out = pl.pallas_call(
        functools.partial(_ffn_kernel, n_f=n_f, has_next=has_next, row_chunk=_tile(tm, FFN_ROW_CHUNK)),
        grid=(n_rows // tm, n_f),
        in_specs=in_specs,
        out_specs=out_specs,
        out_shape=out_shape,
        compiler_params=_cparams("parallel", "arbitrary"),
        name="ffn",
    )(*args)
    return out if has_next else (out[0], None)


def kernel(x, c, ctx, c_ctx, ada_w, ada_b, norm_g, w_in, ret_log_decay, conv_w, sgu_w, sgu_b,
           w_branch, w_out, ffn_w_in, ffn_w_out):
    batch, seq, d = x.shape
    ctx_len = ctx.shape[1]
    depth = ada_w.shape[0]
    d_ff = ffn_w_out.shape[1]
    assert batch + 1 <= MOD_ROWS and w_in.shape[2] == N_MIX_COLS + N_BRANCH * d

    rows = _Rows(batch, seq, ctx_len, _tile(batch * ctx_len, 512))
    tm_big = _tile(batch * ctx_len, 1024)

    cc = jnp.zeros((MOD_ROWS, d), f32).at[:batch].set(c).at[batch].set(c_ctx)
    mods = _adaln(cc, ada_w, ada_b).reshape(depth * MOD_ROWS * 6, 1, d)

    rope_tabs = _rope_tables(seq)
    dft_x = _dft_tables(seq)
    dft_c = _dft_tables(ctx_len)

    xs, hx = _norm_mod(x.reshape(batch * seq, d), ctx.reshape(batch * ctx_len, d),
                       norm_g[0, 0:1], mods, rows)

    tn = _tile(d, 1024)
    for l in range(depth):
        w_mix = _col_blocks(w_in[l, :, :N_MIX_COLS].astype(bf16), _tile(N_MIX_COLS, 1024))
        w_gate = _col_blocks(w_in[l, :, N_MIX_COLS:].astype(bf16), tn)
        w_br = jnp.transpose(w_branch[l].astype(bf16).reshape(N_BRANCH, MIX_W, d // tn, tn), (0, 2, 1, 3))
        w_ffn1 = _col_blocks(ffn_w_in[l].astype(bf16), _tile(d_ff, 512))
        sgu_bias = jnp.repeat(jnp.transpose(sgu_b[l]), GROUP_DIM, axis=1)

        last = l + 1 == depth
        n_rows = rows.m_x if last else rows.m

        proj = _matmul(hx, w_mix, tm_big, "inproj")
        ya = _retention(proj, ret_log_decay[l], rows, rope_tabs, None)
        yb = _fourier(proj, rows, dft_x, None)
        if not last:
            ya = _retention(proj, ret_log_decay[l], rows, None, ya)
            yb = _fourier(proj, rows, dft_c, yb)
        yc = _sgu(proj, sgu_w[l].astype(bf16), sgu_bias, rows, rows.tm, n_rows)
        yd = _conv(proj, conv_w[l], rows, rows.tm, n_rows)
        merged = _merge(hx, (ya, yb, yc, yd), w_gate, w_br, tm_big, n_rows)
        xs, hx2 = _outproj(merged, w_out[l].astype(bf16), xs, norm_g[l, 1:2], norm_g[l, 2:3], mods, l, rows,
                           n_rows)
        g_next = None if last else norm_g[l + 1, 0:1]
        xs, hx = _ffn(hx2, w_ffn1, ffn_w_out[l].astype(bf16), xs, norm_g[l, 3:4], g_next, mods, l, rows, n_rows)

    return xs.reshape(batch, seq, d)
```

```python
import functools

import jax
import jax.numpy as jnp
from jax import lax
from jax.experimental import pallas as pl
from jax.experimental.pallas import tpu as pltpu

f32 = jnp.float32
bf16 = jnp.bfloat16

GRID_W = 64
MIX_W = 512
N_BRANCH = 4
GROUP_DIM = 128
N_GROUPS = MIX_W // GROUP_DIM
CHUNK = 128
ROPE_BASE = 10000.0
EPS = 1e-6
N_MIX_COLS = 10 * MIX_W
COL_F, COL_U, COL_V, COL_SB, COL_SC, COL_SX = 4, 5, 6, 7, 8, 9
MOD_ROWS = 16
CHUNK_UNROLL = 16
FFN_ROW_CHUNK = 256
MERGE_ROW_CHUNK = 512
OUTPROJ_ROW_CHUNK = 256

VMEM_LIMIT_BYTES = 56 * 1024 * 1024


def _cparams(*sem):
    return pltpu.CompilerParams(dimension_semantics=sem, vmem_limit_bytes=VMEM_LIMIT_BYTES)


def _tile(n, pref):
    t = min(n, pref)
    while n % t:
        t //= 2
    return t


def _rms(x):
    return x * lax.rsqrt(jnp.mean(x * x, axis=-1, keepdims=True) + EPS)


def _adaln_kernel(c_ref, w_ref, b_ref, o_ref):
    s = jax.nn.silu(c_ref[...]).astype(bf16)
    o_ref[...] = jnp.dot(s, w_ref[...].astype(bf16), preferred_element_type=f32) + b_ref[...]


def _adaln(cc, ada_w, ada_b):
    depth, d, n6 = ada_w.shape
    tn = _tile(n6, 1024)
    return pl.pallas_call(
        _adaln_kernel,
        grid=(depth, n6 // tn),
        in_specs=[
            pl.BlockSpec((MOD_ROWS, d), lambda l, j: (0, 0)),
            pl.BlockSpec((None, d, tn), lambda l, j: (l, 0, j)),
            pl.BlockSpec((None, 1, tn), lambda l, j: (l, 0, j)),
        ],
        out_specs=pl.BlockSpec((None, MOD_ROWS, tn), lambda l, j: (l, 0, j)),
        out_shape=jax.ShapeDtypeStruct((depth, MOD_ROWS, n6), f32),
        compiler_params=_cparams("parallel", "parallel"),
        name="adaln",
    )(cc, ada_w, ada_b.reshape(depth, 1, n6))


class _Rows:
    def __init__(self, batch, seq, ctx_len, tm):
        self.batch, self.seq, self.ctx_len, self.tm = batch, seq, ctx_len, tm
        self.m_x, self.m_c = batch * seq, batch * ctx_len
        self.m = self.m_x + self.m_c
        assert seq % tm == 0 and self.m_c % tm == 0
        self.n_x_tiles = self.m_x // tm
        self.n_tiles = self.m // tm
        self.per_batch = seq // tm

    def mod_spec(self, layer, k, d, extra_axes=0):
        def index(i, *_):
            r = jnp.where(i < self.n_x_tiles, i // self.per_batch, self.batch)
            return ((layer * MOD_ROWS + r) * 6 + k, 0, 0)
        return pl.BlockSpec((None, 1, d), index)


def _vec_spec(d):
    return pl.BlockSpec((1, d), lambda i, *_: (0, 0))


def _norm_mod_kernel(x_ref, c_ref, g_ref, sh_ref, sc_ref, xs_ref, hx_ref, *, n_x_tiles):
    def emit(x):
        xs_ref[...] = x
        hx_ref[...] = (_rms(x) * g_ref[...] * (1.0 + sc_ref[...]) + sh_ref[...]).astype(bf16)

    @pl.when(pl.program_id(0) < n_x_tiles)
    def _():
        emit(x_ref[...])

    @pl.when(pl.program_id(0) >= n_x_tiles)
    def _():
        emit(c_ref[...])


def _norm_mod(x2d, c2d, g, mods, rows):
    d = x2d.shape[1]
    tm, nxt = rows.tm, rows.n_x_tiles
    return pl.pallas_call(
        functools.partial(_norm_mod_kernel, n_x_tiles=nxt),
        grid=(rows.n_tiles,),
        in_specs=[
            pl.BlockSpec((tm, d), lambda i: (jnp.minimum(i, nxt - 1), 0)),
            pl.BlockSpec((tm, d), lambda i: (jnp.maximum(i - nxt, 0), 0)),
            _vec_spec(d),
            rows.mod_spec(0, 0, d),
            rows.mod_spec(0, 1, d),
        ],
        out_specs=[pl.BlockSpec((tm, d), lambda i: (i, 0)), pl.BlockSpec((tm, d), lambda i: (i, 0))],
        out_shape=[jax.ShapeDtypeStruct((rows.m, d), f32), jax.ShapeDtypeStruct((rows.m, d), bf16)],
        compiler_params=_cparams("parallel"),
        name="norm_mod",
    )(x2d, c2d, g, mods, mods)


def _matmul_kernel(a_ref, b_ref, o_ref):
    o_ref[...] = jnp.dot(a_ref[...], b_ref[...], preferred_element_type=f32).astype(o_ref.dtype)


def _matmul(a, b, tm, tn, name):
    m, k = a.shape
    n = b.shape[1]
    return pl.pallas_call(
        _matmul_kernel,
        grid=(m // tm, n // tn),
        in_specs=[pl.BlockSpec((tm, k), lambda i, j: (i, 0)), pl.BlockSpec((k, tn), lambda i, j: (0, j))],
        out_specs=pl.BlockSpec((tm, tn), lambda i, j: (i, j)),
        out_shape=jax.ShapeDtypeStruct((m, n), bf16),
        compiler_params=_cparams("parallel", "arbitrary"),
        name=name,
    )(a, b)


def _rope_tables(seq):
    t = jnp.arange(seq)
    n = GROUP_DIM // 2
    freqs = ROPE_BASE ** (-jnp.arange(0, n, 2, dtype=f32) / n)

    def tab(pos):
        ang = pos.astype(f32)[:, None] * freqs[None, :]
        c, s = jnp.cos(ang), jnp.sin(ang)
        return jnp.concatenate([c, c], -1), jnp.concatenate([-s, s], -1)

    c1, s1 = tab(t // GRID_W)
    c2, s2 = tab(t % GRID_W)
    dst = jnp.arange(GROUP_DIM)
    src = jnp.where((dst % n) < n // 2, dst + n // 2, dst - n // 2)
    perm = (jnp.arange(GROUP_DIM)[:, None] == src[None, :]).astype(bf16)
    return jnp.concatenate([c1, c2], -1), jnp.concatenate([s1, s2], -1), perm


def _retention_kernel(*refs, n_chunks, rope, ctx_len):
    refs = list(refs)
    lg_ref, q_ref, k_ref, v_ref, g_ref = refs[:5]
    pos = 5
    if ctx_len:
        kc_ref, vc_ref = refs[pos:pos + 2]
        pos += 2
    if rope:
        cos_ref, sin_ref, perm_ref = refs[pos:pos + 3]
        pos += 3
    o_ref, qs, ks, sf, sb = refs[pos:]

    c_len, dh = CHUNK, GROUP_DIM
    h = pl.program_id(1)
    lgf, lgb = lg_ref[0, h], lg_ref[1, h]
    scale = dh ** -0.5
    ri = lax.broadcasted_iota(jnp.int32, (c_len, dh), 0).astype(f32)
    ci = lax.broadcasted_iota(jnp.int32, (c_len, dh), 1).astype(f32)
    rel = ri - ci
    decay = (jnp.where(rel >= 0, jnp.exp(lgf * jnp.maximum(rel, 0.0)), 0.0)
             + jnp.where(rel <= 0, jnp.exp(lgb * jnp.maximum(-rel, 0.0)), 0.0))
    kw_f = jnp.exp(lgf * (c_len - 1 - ri))
    kw_b = jnp.exp(lgb * ri)
    qw_f = jnp.exp(lgf * (ri + 1.0))
    qw_b = jnp.exp(lgb * (c_len - ri))
    cd_f = jnp.exp(jnp.full((dh, dh), lgf * c_len, f32))
    cd_b = jnp.exp(jnp.full((dh, dh), lgb * c_len, f32))

    def rows_of(c):
        return pl.ds(pl.multiple_of(c * c_len, c_len), c_len)

    def kt_dot(kb, vw):
        return lax.dot_general(kb, vw, (((0,), (0,)), ((), ())), preferred_element_type=f32)

    def prep(c, carry):
        rows = rows_of(c)
        q, k = q_ref[rows, :], k_ref[rows, :]
        qf, kf = q.astype(f32), k.astype(f32)
        if rope:
            cos, sin = cos_ref[rows, :], sin_ref[rows, :]
            qf = qf * cos + jnp.dot(q, perm_ref[...], preferred_element_type=f32) * sin
            kf = kf * cos + jnp.dot(k, perm_ref[...], preferred_element_type=f32) * sin
        kb = (kf * scale).astype(bf16)
        qs[rows, :] = qf.astype(bf16)
        ks[rows, :] = kb
        v = v_ref[rows, :].astype(f32)
        vw = jnp.concatenate([v * kw_f, v * kw_b], axis=1).astype(bf16)
        ds = kt_dot(kb, vw)
        sf[c] = ds[:, :dh]
        sb[c] = ds[:, dh:]
        return carry

    unroll = min(n_chunks, CHUNK_UNROLL)
    lax.fori_loop(0, n_chunks, prep, 0, unroll=unroll)

    if ctx_len:
        mi = lax.broadcasted_iota(jnp.int32, (ctx_len, dh), 0).astype(f32)
        kcb = (kc_ref[...].astype(f32) * scale).astype(bf16)
        vc = vc_ref[...].astype(f32)
        vcw = jnp.concatenate([vc * jnp.exp(lgf * (ctx_len - 1 - mi)), vc * jnp.exp(lgb * mi)],
                              axis=1).astype(bf16)
        s0 = kt_dot(kcb, vcw)
        s0_f, s0_b = s0[:, :dh], s0[:, dh:]
    else:
        s0_f = s0_b = jnp.zeros((dh, dh), f32)

    def scan_f(c, s):
        nxt = cd_f * s + sf[c]
        sf[c] = s
        return nxt

    def scan_b(j, s):
        c = n_chunks - 1 - j
        nxt = cd_b * s + sb[c]
        sb[c] = s
        return nxt

    lax.fori_loop(0, n_chunks, scan_f, s0_f)
    lax.fori_loop(0, n_chunks, scan_b, s0_b)

    def emit(c, carry):
        rows = rows_of(c)
        q, k, v = qs[rows, :], ks[rows, :], v_ref[rows, :]
        sc = lax.dot_general(q, k, (((1,), (1,)), ((), ())), preferred_element_type=f32) * decay
        inner = jnp.dot(sc.astype(bf16), v, preferred_element_type=f32)
        s2 = jnp.concatenate([sf[c], sb[c]], axis=1).astype(bf16)
        cr = jnp.dot(q, s2, preferred_element_type=f32)
        ret = inner + qw_f * cr[:, :dh] + qw_b * cr[:, dh:]
        g = g_ref[rows, :].astype(f32)
        o_ref[rows, :] = (jax.nn.silu(g) * _rms(ret)).astype(bf16)
        return carry

    lax.fori_loop(0, n_chunks, emit, 0, unroll=unroll)


def _retention(proj, log_decay, rows, rope_tabs, prev_out):
    latent = prev_out is None
    seq = rows.seq if latent else rows.ctx_len
    n_chunks = seq // CHUNK
    row0 = 0 if latent else rows.m_x // seq
    hq = MIX_W // GROUP_DIM

    def col(off):
        return pl.BlockSpec((seq, GROUP_DIM), lambda b, h: (row0 + b, off * hq + h))

    in_specs = [pl.BlockSpec(memory_space=pltpu.SMEM), col(0), col(1), col(2), col(3)]
    args = [log_decay, proj, proj, proj, proj]
    if latent:
        cb0 = rows.m_x // rows.ctx_len
        for off in (1, 2):
            in_specs.append(pl.BlockSpec((rows.ctx_len, GROUP_DIM),
                                         lambda b, h, off=off: (cb0 + b, off * hq + h)))
            args.append(proj)
        cos, sin, perm = rope_tabs
        in_specs += [pl.BlockSpec((seq, GROUP_DIM), lambda b, h: (0, 0)),
                     pl.BlockSpec((seq, GROUP_DIM), lambda b, h: (0, 0)),
                     pl.BlockSpec((GROUP_DIM, GROUP_DIM), lambda b, h: (0, 0))]
        args += [cos, sin, perm]
        aliases = {}
    else:
        in_specs.append(pl.BlockSpec(memory_space=pl.ANY))
        args.append(prev_out)
        aliases = {len(args) - 1: 0}

    kernel = functools.partial(_retention_kernel, n_chunks=n_chunks, rope=latent,
                               ctx_len=rows.ctx_len if latent else 0)
    if not latent:
        def kernel(*refs, _k=kernel):
            return _k(*refs[:5], *refs[6:])

    return pl.pallas_call(
        kernel,
        grid=(rows.batch, hq),
        in_specs=in_specs,
        out_specs=pl.BlockSpec((seq, GROUP_DIM), lambda b, h: (row0 + b, h)),
        out_shape=jax.ShapeDtypeStruct((rows.m, MIX_W), bf16),
        scratch_shapes=[pltpu.VMEM((seq, GROUP_DIM), bf16), pltpu.VMEM((seq, GROUP_DIM), bf16),
                        pltpu.VMEM((n_chunks, GROUP_DIM, GROUP_DIM), f32),
                        pltpu.VMEM((n_chunks, GROUP_DIM, GROUP_DIM), f32)],
        input_output_aliases=aliases,
        compiler_params=_cparams("parallel", "parallel"),
        name="retention_x" if latent else "retention_ctx",
    )(*args)


FOURIER_TILE = 512
FOURIER_EXTRA = 16


def _dft_tables(seq):
    def phase(k, n):
        ang = (k % n).astype(f32) * (2.0 * jnp.pi / n)
        return jnp.cos(ang), jnp.sin(ang)

    def cs(n_f, n):
        r = 1 << ((n_f.bit_length() - 1) // 2)
        t = jnp.arange(n, dtype=jnp.int32)
        ca, sa = phase((r * jnp.arange(n_f // r, dtype=jnp.int32))[:, None] * t[None, :], n)
        cb, sb = phase(jnp.arange(r, dtype=jnp.int32)[:, None] * t[None, :], n)
        cos = ca[:, None, :] * cb[None] - sa[:, None, :] * sb[None]
        sin = sa[:, None, :] * cb[None] + ca[:, None, :] * sb[None]
        return cos.reshape(n_f, n), sin.reshape(n_f, n)

    half = seq // 2
    tm = _tile(half, FOURIER_TILE)
    n_tiles = half // tm
    cl, sl = cs(half, seq)
    nyq = jnp.where(jnp.arange(seq) % 2 == 0, 1.0, -1.0).astype(f32)
    extra = jnp.zeros((n_tiles, FOURIER_EXTRA, seq), f32).at[:, 0, :].set(nyq)
    pos = jnp.concatenate([cl.reshape(n_tiles, tm, seq), sl.reshape(n_tiles, tm, seq), extra], axis=1)

    cc, sc = cs(GROUP_DIM, GROUP_DIM)
    norm = (seq * GROUP_DIM) ** -0.5
    mix = jnp.concatenate([jnp.concatenate([cc, cc], 1), jnp.concatenate([-sc, sc], 1)], 0) * norm

    s = jnp.arange(tm)
    flip = ((s[:, None] + s[None, :] == tm) & (s[:, None] >= 1)).astype(bf16)
    return pos.astype(bf16), mix.astype(bf16), flip


def _fourier_kernel(z_ref, pos_ref, mix_ref, flip_ref, o_ref, stash, *, seq, tm, n_tiles):
    j = pl.program_id(1)
    m = n_tiles - 1 - j

    def mixed(cz, sz):
        qs, ps = [], []
        for g in range(N_GROUPS):
            cols = slice(g * GROUP_DIM, (g + 1) * GROUP_DIM)
            szg = jnp.zeros_like(cz[:, cols]) if sz is None else sz[:, cols]
            lhs = jnp.concatenate([cz[:, cols], szg], axis=1).astype(bf16)
            out = jnp.dot(lhs, mix_ref[...], preferred_element_type=f32)
            qs.append(out[:, :GROUP_DIM])
            ps.append(out[:, GROUP_DIM:])
        return jnp.concatenate(qs, axis=1), jnp.concatenate(ps, axis=1)

    r = jnp.dot(pos_ref[...], z_ref[...], preferred_element_type=f32)
    q, p = mixed(r[:tm], r[tm:2 * tm])

    @pl.when(j == 0)
    def _():
        stash[...] = mixed(r[2 * tm:], None)[1]

    above = stash[0:1, :]
    o_ref[pl.ds(pl.multiple_of(m * tm, tm), tm), :] = q.astype(bf16)
    mir = jnp.dot(flip_ref[...], p.astype(bf16), preferred_element_type=f32)
    row = lax.broadcasted_iota(jnp.int32, mir.shape, 0)
    mir = jnp.where(row == 0, above, mir)
    o_ref[pl.ds(pl.multiple_of(seq // 2 + j * tm, tm), tm), :] = mir.astype(bf16)
    stash[...] = p[:FOURIER_EXTRA]


def _fourier(proj, rows, tables, prev_out):
    latent = prev_out is None
    seq = rows.seq if latent else rows.ctx_len
    pos_tab, mix_tab, flip_tab = tables
    n_tiles, tab_rows, _ = pos_tab.shape
    tm = flip_tab.shape[0]
    row0 = 0 if latent else rows.m_x // seq
    in_specs = [pl.BlockSpec((seq, MIX_W), lambda b, j: (row0 + b, COL_F)),
                pl.BlockSpec((None, tab_rows, seq), lambda b, j: (n_tiles - 1 - j, 0, 0)),
                pl.BlockSpec((2 * GROUP_DIM, 2 * GROUP_DIM), lambda b, j: (0, 0)),
                pl.BlockSpec((tm, tm), lambda b, j: (0, 0))]
    args = [proj, pos_tab, mix_tab, flip_tab]
    kernel = functools.partial(_fourier_kernel, seq=seq, tm=tm, n_tiles=n_tiles)
    aliases = {}
    if not latent:
        in_specs.append(pl.BlockSpec(memory_space=pl.ANY))
        args.append(prev_out)
        aliases = {4: 0}

        def kernel(z_ref, pos_ref, mix_ref, flip_ref, _carrier, o_ref, stash, _k=kernel):
            return _k(z_ref, pos_ref, mix_ref, flip_ref, o_ref, stash)

    return pl.pallas_call(
        kernel,
        grid=(rows.batch, n_tiles),
        in_specs=in_specs,
        out_specs=pl.BlockSpec((seq, MIX_W), lambda b, j: (row0 + b, 0)),
        out_shape=jax.ShapeDtypeStruct((rows.m, MIX_W), bf16),
        scratch_shapes=[pltpu.VMEM((FOURIER_EXTRA, MIX_W), f32)],
        input_output_aliases=aliases,
        compiler_params=_cparams("parallel", "arbitrary"),
        name="fourier_x" if latent else "fourier_ctx",
    )(*args)


def _sgu_kernel(u_ref, v_ref, w_ref, b_ref, o_ref, *, tm):
    for ch in range(tm // CHUNK):
        rows = slice(ch * CHUNK, (ch + 1) * CHUNK)
        u = jax.nn.gelu(u_ref[rows, :].astype(f32))
        v = jax.nn.gelu(v_ref[rows, :].astype(f32))
        outs = []
        for g in range(N_GROUPS):
            cols = slice(g * GROUP_DIM, (g + 1) * GROUP_DIM)
            vg = v[:, cols]
            dv = vg - jnp.mean(vg, axis=-1, keepdims=True)
            vn = dv * lax.rsqrt(jnp.mean(dv * dv, axis=-1, keepdims=True) + EPS)
            s = jnp.dot(w_ref[g], vn.astype(bf16), preferred_element_type=f32) + b_ref[:, cols]
            outs.append(u[:, cols] * s)
        o_ref[rows, :] = jnp.concatenate(outs, axis=1).astype(bf16)


def _sgu(proj, sgu_w, sgu_bias, rows, tm, n_rows):
    return pl.pallas_call(
        functools.partial(_sgu_kernel, tm=tm),
        grid=(n_rows // tm,),
        in_specs=[pl.BlockSpec((tm, MIX_W), lambda i: (i, COL_U)),
                  pl.BlockSpec((tm, MIX_W), lambda i: (i, COL_V)),
                  pl.BlockSpec((N_GROUPS, CHUNK, CHUNK), lambda i: (0, 0, 0)),
                  pl.BlockSpec((CHUNK, MIX_W), lambda i: (0, 0))],
        out_specs=pl.BlockSpec((tm, MIX_W), lambda i: (i, 0)),
        out_shape=jax.ShapeDtypeStruct((rows.m, MIX_W), bf16),
        compiler_params=_cparams("parallel"),
        name="sgu",
    )(proj, proj, sgu_w, sgu_bias)


def _conv_kernel(b_ref, c_ref, x_ref, w_ref, o_ref, *, tm, n_x_tiles, ctx_len):
    y = c_ref[...].astype(f32) * x_ref[...].astype(f32)
    r = lax.broadcasted_iota(jnp.int32, y.shape, 0)
    period = jnp.where(pl.program_id(0) < n_x_tiles, GRID_W, ctx_len)
    rp = r & (period - 1)
    prev = jnp.where(rp == 0, 0.0, pltpu.roll(y, 1, axis=0))
    nxt = jnp.where(rp == period - 1, 0.0, pltpu.roll(y, tm - 1, axis=0))
    conv = w_ref[0:1, :] * prev + w_ref[1:2, :] * y + w_ref[2:3, :] * nxt
    o_ref[...] = (b_ref[...].astype(f32) * conv).astype(bf16)


def _conv(proj, conv_w, rows, tm, n_rows):
    assert GRID_W & (GRID_W - 1) == 0 and rows.ctx_len & (rows.ctx_len - 1) == 0
    assert tm % GRID_W == 0 and tm % rows.ctx_len == 0
    return pl.pallas_call(
        functools.partial(_conv_kernel, tm=tm, n_x_tiles=rows.m_x // tm, ctx_len=rows.ctx_len),
        grid=(n_rows // tm,),
        in_specs=[pl.BlockSpec((tm, MIX_W), lambda i: (i, COL_SB)),
                  pl.BlockSpec((tm, MIX_W), lambda i: (i, COL_SC)),
                  pl.BlockSpec((tm, MIX_W), lambda i: (i, COL_SX)),
                  pl.BlockSpec((3, MIX_W), lambda i: (0, 0))],
        out_specs=pl.BlockSpec((tm, MIX_W), lambda i: (i, 0)),
        out_shape=jax.ShapeDtypeStruct((rows.m, MIX_W), bf16),
        compiler_params=_cparams("parallel"),
        name="conv",
    )(proj, proj, proj, conv_w)


def _merge_kernel(hx_ref, ya_ref, yb_ref, yc_ref, yd_ref, wg_ref, wb_ref, o_ref, acc, *, row_chunk):
    n = pl.program_id(2)
    for k, y_ref in enumerate((ya_ref, yb_ref, yc_ref, yd_ref)):
        @pl.when(n == k)
        def _(k=k, y_ref=y_ref):
            for r in range(hx_ref.shape[0] // row_chunk):
                rows = slice(r * row_chunk, (r + 1) * row_chunk)
                gate = jax.nn.sigmoid(jnp.dot(hx_ref[rows, :], wg_ref[...], preferred_element_type=f32))
                term = gate * jnp.dot(y_ref[rows, :], wb_ref[...], preferred_element_type=f32)
                if k == 0:
                    acc[rows, :] = term
                elif k < N_BRANCH - 1:
                    acc[rows, :] += term
                else:
                    o_ref[rows, :] = (acc[rows, :] + term).astype(bf16)


def _merge(hx, ys, w_gate, w_branch, tm, tn, n_rows):
    d = hx.shape[1]
    m = n_rows
    nj = d // tn
    y_spec = pl.BlockSpec((tm, MIX_W), lambda i, j, n: (i, 0))
    return pl.pallas_call(
        functools.partial(_merge_kernel, row_chunk=_tile(tm, MERGE_ROW_CHUNK)),
        grid=(m // tm, nj, N_BRANCH),
        in_specs=[pl.BlockSpec((tm, d), lambda i, j, n: (i, 0)), y_spec, y_spec, y_spec, y_spec,
                  pl.BlockSpec((d, tn), lambda i, j, n: (0, n * nj + j)),
                  pl.BlockSpec((None, MIX_W, tn), lambda i, j, n: (n, 0, j))],
        out_specs=pl.BlockSpec((tm, tn), lambda i, j, n: (i, j)),
        out_shape=jax.ShapeDtypeStruct((m, d), bf16),
        scratch_shapes=[pltpu.VMEM((tm, tn), f32)],
        compiler_params=_cparams("parallel", "arbitrary", "arbitrary"),
        name="merge",
    )(hx, *ys, w_gate, w_branch)


def _outproj_kernel(m_ref, w_ref, x_ref, g1_ref, g2_ref, gate_ref, sh_ref, sc_ref, xo_ref, ho_ref, *, row_chunk):
    res_scale = gate_ref[...] * g1_ref[...]
    mod_scale = g2_ref[...] * (1.0 + sc_ref[...])
    for r in range(m_ref.shape[0] // row_chunk):
        rows = slice(r * row_chunk, (r + 1) * row_chunk)
        mix = jnp.dot(m_ref[rows, :], w_ref[...], preferred_element_type=f32)
        xn = x_ref[rows, :] + _rms(mix) * res_scale
        xo_ref[rows, :] = xn
        ho_ref[rows, :] = (_rms(xn) * mod_scale + sh_ref[...]).astype(bf16)


def _outproj(merged, w_out, xs, g1, g2, mods, layer, rows, n_rows):
    d = xs.shape[1]
    tm = rows.tm
    row = pl.BlockSpec((tm, d), lambda i: (i, 0))
    return pl.pallas_call(
        functools.partial(_outproj_kernel, row_chunk=_tile(tm, OUTPROJ_ROW_CHUNK)),
        grid=(n_rows // tm,),
        in_specs=[row, pl.BlockSpec((d, d), lambda i: (0, 0)), row, _vec_spec(d), _vec_spec(d),
                  rows.mod_spec(layer, 2, d), rows.mod_spec(layer, 3, d), rows.mod_spec(layer, 4, d)],
        out_specs=[row, row],
        out_shape=[jax.ShapeDtypeStruct((n_rows, d), f32), jax.ShapeDtypeStruct((n_rows, d), bf16)],
        compiler_params=_cparams("parallel"),
        name="outproj",
    )(merged, w_out, xs, g1, g2, mods, mods, mods)


def _ffn_kernel(*refs, n_f, has_next, row_chunk):
    if has_next:
        h_ref, wa_ref, wb_ref, w2_ref, x_ref, g3_ref, gate_ref, gn_ref, sh_ref, sc_ref, xo_ref, ho_ref = refs
    else:
        h_ref, wa_ref, wb_ref, w2_ref, x_ref, g3_ref, gate_ref, xo_ref = refs
    f = pl.program_id(1)

    def step(first, last):
        if last:
            res_scale = gate_ref[...] * g3_ref[...]
            if has_next:
                mod_scale = gn_ref[...] * (1.0 + sc_ref[...])
        for r in range(h_ref.shape[0] // row_chunk):
            rows = slice(r * row_chunk, (r + 1) * row_chunk)
            h = h_ref[rows, :]
            a = jnp.dot(h, wa_ref[...], preferred_element_type=f32)
            b = jnp.dot(h, wb_ref[...], preferred_element_type=f32)
            acc = jnp.dot((jax.nn.silu(a) * b).astype(bf16), w2_ref[...], preferred_element_type=f32)
            if not first:
                acc = xo_ref[rows, :] + acc
            if last:
                acc = x_ref[rows, :] + _rms(acc) * res_scale
                if has_next:
                    ho_ref[rows, :] = (_rms(acc) * mod_scale + sh_ref[...]).astype(bf16)
            xo_ref[rows, :] = acc

    if n_f == 1:
        step(True, True)
    else:
        pl.when(f == 0)(lambda: step(True, False))
        pl.when(f == n_f - 1)(lambda: step(False, True))
        if n_f > 2:
            pl.when((f > 0) & (f < n_f - 1))(lambda: step(False, False))


def _ffn(hx, w1, w2, xs, g3, g_next, mods, layer, rows, tf, n_rows):
    d = xs.shape[1]
    n_f = w2.shape[0] // tf
    tm = rows.tm
    has_next = g_next is not None
    row = pl.BlockSpec((tm, d), lambda i, f: (i, 0))
    in_specs = [row,
                pl.BlockSpec((d, tf), lambda i, f: (0, f)),
                pl.BlockSpec((d, tf), lambda i, f: (0, n_f + f)),
                pl.BlockSpec((tf, d), lambda i, f: (f, 0)),
                row, _vec_spec(d), rows.mod_spec(layer, 5, d)]
    args = [hx, w1, w1, w2, xs, g3, mods]
    out_specs = [row]
    out_shape = [jax.ShapeDtypeStruct((n_rows, d), f32)]
    if has_next:
        in_specs += [_vec_spec(d), rows.mod_spec(layer + 1, 0, d), rows.mod_spec(layer + 1, 1, d)]
        args += [g_next, mods, mods]
        out_specs.append(row)
        out_shape.append(jax.ShapeDtypeStruct((n_rows, d), bf16))
    out = pl.pallas_call(
        functools.partial(_ffn_kernel, n_f=n_f, has_next=has_next, row_chunk=_tile(tm, FFN_ROW_CHUNK)),
        grid=(n_rows // tm, n_f),
        in_specs=in_specs,
        out_specs=out_specs,
        out_shape=out_shape,
        compiler_params=_cparams("parallel", "arbitrary"),
        name="ffn",
    )(*args)
    return out if has_next else (out[0], None)


def kernel(x, c, ctx, c_ctx, ada_w, ada_b, norm_g, w_in, ret_log_decay, conv_w, sgu_w, sgu_b,
           w_branch, w_out, ffn_w_in, ffn_w_out):
    batch, seq, d = x.shape
    ctx_len = ctx.shape[1]
    depth = ada_w.shape[0]
    d_ff = ffn_w_out.shape[1]
    assert batch + 1 <= MOD_ROWS and w_in.shape[2] == N_MIX_COLS + N_BRANCH * d

    rows = _Rows(batch, seq, ctx_len, _tile(batch * ctx_len, 512))
    tm_big = _tile(batch * ctx_len, 1024)

    cc = jnp.zeros((MOD_ROWS, d), f32).at[:batch].set(c).at[batch].set(c_ctx)
    mods = _adaln(cc, ada_w, ada_b).reshape(depth * MOD_ROWS * 6, 1, d)

    rope_tabs = _rope_tables(seq)
    dft_x = _dft_tables(seq)
    dft_c = _dft_tables(ctx_len)

    xs, hx = _norm_mod(x.reshape(batch * seq, d), ctx.reshape(batch * ctx_len, d),
                       norm_g[0, 0:1], mods, rows)

    for l in range(depth):
        w_mix = w_in[l, :, :N_MIX_COLS].astype(bf16)
        w_gate = w_in[l, :, N_MIX_COLS:].astype(bf16)
        sgu_bias = jnp.repeat(jnp.transpose(sgu_b[l]), GROUP_DIM, axis=1)

        last = l + 1 == depth
        n_rows = rows.m_x if last else rows.m

        proj = _matmul(hx, w_mix, tm_big, _tile(N_MIX_COLS, 1024), "inproj")
        ya = _retention(proj, ret_log_decay[l], rows, rope_tabs, None)
        yb = _fourier(proj, rows, dft_x, None)
        if not last:
            ya = _retention(proj, ret_log_decay[l], rows, None, ya)
            yb = _fourier(proj, rows, dft_c, yb)
        yc = _sgu(proj, sgu_w[l].astype(bf16), sgu_bias, rows, rows.tm, n_rows)
        yd = _conv(proj, conv_w[l], rows, rows.tm, n_rows)
        merged = _merge(hx, (ya, yb, yc, yd), w_gate, w_branch[l].astype(bf16), tm_big, _tile(d, 1024), n_rows)
        xs, hx2 = _outproj(merged, w_out[l].astype(bf16), xs, norm_g[l, 1:2], norm_g[l, 2:3], mods, l, rows,
                           n_rows)
        g_next = None if last else norm_g[l + 1, 0:1]
        xs, hx = _ffn(hx2, ffn_w_in[l].astype(bf16), ffn_w_out[l].astype(bf16), xs, norm_g[l, 3:4],
                      g_next, mods, l, rows, _tile(d_ff, 512), n_rows)

    return xs.reshape(batch, seq, d)
```

```python
import functools

import jax
import jax.numpy as jnp
from jax import lax
from jax.experimental import pallas as pl
from jax.experimental.pallas import tpu as pltpu

f32 = jnp.float32
bf16 = jnp.bfloat16

GRID_W = 64
MIX_W = 512
N_BRANCH = 4
GROUP_DIM = 128
N_GROUPS = MIX_W // GROUP_DIM
CHUNK = 128
ROPE_BASE = 10000.0
EPS = 1e-6
N_MIX_COLS = 10 * MIX_W
COL_F, COL_U, COL_V, COL_SB, COL_SC, COL_SX = 4, 5, 6, 7, 8, 9
MOD_ROWS = 16
CHUNK_UNROLL = 16
FFN_ROW_CHUNK = 256
MERGE_ROW_CHUNK = 512
OUTPROJ_ROW_CHUNK = 256

VMEM_LIMIT_BYTES = 56 * 1024 * 1024


def _cparams(*sem):
    return pltpu.CompilerParams(dimension_semantics=sem, vmem_limit_bytes=VMEM_LIMIT_BYTES)


def _tile(n, pref):
    t = min(n, pref)
    while n % t:
        t //= 2
    return t


def _rms(x):
    return x * lax.rsqrt(jnp.mean(x * x, axis=-1, keepdims=True) + EPS)


def _adaln_kernel(c_ref, w_ref, b_ref, o_ref):
    s = jax.nn.silu(c_ref[...]).astype(bf16)
    o_ref[...] = jnp.dot(s, w_ref[...].astype(bf16), preferred_element_type=f32) + b_ref[...]


def _adaln(cc, ada_w, ada_b):
    depth, d, n6 = ada_w.shape
    tn = _tile(n6, 1024)
    return pl.pallas_call(
        _adaln_kernel,
        grid=(depth, n6 // tn),
        in_specs=[
            pl.BlockSpec((MOD_ROWS, d), lambda l, j: (0, 0)),
            pl.BlockSpec((None, d, tn), lambda l, j: (l, 0, j)),
            pl.BlockSpec((None, 1, tn), lambda l, j: (l, 0, j)),
        ],
        out_specs=pl.BlockSpec((None, MOD_ROWS, tn), lambda l, j: (l, 0, j)),
        out_shape=jax.ShapeDtypeStruct((depth, MOD_ROWS, n6), f32),
        compiler_params=_cparams("parallel", "parallel"),
        name="adaln",
    )(cc, ada_w, ada_b.reshape(depth, 1, n6))


class _Rows:
    def __init__(self, batch, seq, ctx_len, tm):
        self.batch, self.seq, self.ctx_len, self.tm = batch, seq, ctx_len, tm
        self.m_x, self.m_c = batch * seq, batch * ctx_len
        self.m = self.m_x + self.m_c
        assert seq % tm == 0 and self.m_c % tm == 0
        self.n_x_tiles = self.m_x // tm
        self.n_tiles = self.m // tm
        self.per_batch = seq // tm

    def mod_spec(self, layer, k, d, extra_axes=0):
        def index(i, *_):
            r = jnp.where(i < self.n_x_tiles, i // self.per_batch, self.batch)
            return ((layer * MOD_ROWS + r) * 6 + k, 0, 0)
        return pl.BlockSpec((None, 1, d), index)


def _vec_spec(d):
    return pl.BlockSpec((1, d), lambda i, *_: (0, 0))


def _norm_mod_kernel(x_ref, c_ref, g_ref, sh_ref, sc_ref, xs_ref, hx_ref, *, n_x_tiles):
    def emit(x):
        xs_ref[...] = x
        hx_ref[...] = (_rms(x) * g_ref[...] * (1.0 + sc_ref[...]) + sh_ref[...]).astype(bf16)

    @pl.when(pl.program_id(0) < n_x_tiles)
    def _():
        emit(x_ref[...])

    @pl.when(pl.program_id(0) >= n_x_tiles)
    def _():
        emit(c_ref[...])


def _norm_mod(x2d, c2d, g, mods, rows):
    d = x2d.shape[1]
    tm, nxt = rows.tm, rows.n_x_tiles
    return pl.pallas_call(
        functools.partial(_norm_mod_kernel, n_x_tiles=nxt),
        grid=(rows.n_tiles,),
        in_specs=[
            pl.BlockSpec((tm, d), lambda i: (jnp.minimum(i, nxt - 1), 0)),
            pl.BlockSpec((tm, d), lambda i: (jnp.maximum(i - nxt, 0), 0)),
            _vec_spec(d),
            rows.mod_spec(0, 0, d),
            rows.mod_spec(0, 1, d),
        ],
        out_specs=[pl.BlockSpec((tm, d), lambda i: (i, 0)), pl.BlockSpec((tm, d), lambda i: (i, 0))],
        out_shape=[jax.ShapeDtypeStruct((rows.m, d), f32), jax.ShapeDtypeStruct((rows.m, d), bf16)],
        compiler_params=_cparams("parallel"),
        name="norm_mod",
    )(x2d, c2d, g, mods, mods)


def _matmul_kernel(a_ref, b_ref, o_ref):
    o_ref[...] = jnp.dot(a_ref[...], b_ref[...], preferred_element_type=f32).astype(o_ref.dtype)


def _matmul(a, b_stack, layer, n, tm, tn, name):
    m, k = a.shape
    return pl.pallas_call(
        _matmul_kernel,
        grid=(m // tm, n // tn),
        in_specs=[pl.BlockSpec((tm, k), lambda i, j: (i, 0)),
                  pl.BlockSpec((None, k, tn), lambda i, j: (layer, 0, j))],
        out_specs=pl.BlockSpec((tm, tn), lambda i, j: (i, j)),
        out_shape=jax.ShapeDtypeStruct((m, n), bf16),
        compiler_params=_cparams("parallel", "arbitrary"),
        name=name,
    )(a, b_stack)


def _rope_tables(seq):
    t = jnp.arange(seq)
    n = GROUP_DIM // 2
    freqs = ROPE_BASE ** (-jnp.arange(0, n, 2, dtype=f32) / n)

    def tab(pos):
        ang = pos.astype(f32)[:, None] * freqs[None, :]
        c, s = jnp.cos(ang), jnp.sin(ang)
        return jnp.concatenate([c, c], -1), jnp.concatenate([-s, s], -1)

    c1, s1 = tab(t // GRID_W)
    c2, s2 = tab(t % GRID_W)
    dst = jnp.arange(GROUP_DIM)
    src = jnp.where((dst % n) < n // 2, dst + n // 2, dst - n // 2)
    perm = (jnp.arange(GROUP_DIM)[:, None] == src[None, :]).astype(bf16)
    return jnp.concatenate([c1, c2], -1), jnp.concatenate([s1, s2], -1), perm


def _retention_kernel(*refs, n_chunks, rope, ctx_len):
    refs = list(refs)
    lg_ref, q_ref, k_ref, v_ref, g_ref = refs[:5]
    pos = 5
    if ctx_len:
        kc_ref, vc_ref = refs[pos:pos + 2]
        pos += 2
    if rope:
        cos_ref, sin_ref, perm_ref = refs[pos:pos + 3]
        pos += 3
    o_ref, qs, ks, sf, sb = refs[pos:]

    c_len, dh = CHUNK, GROUP_DIM
    h = pl.program_id(1)
    lgf, lgb = lg_ref[0, h], lg_ref[1, h]
    scale = dh ** -0.5
    ri = lax.broadcasted_iota(jnp.int32, (c_len, dh), 0).astype(f32)
    ci = lax.broadcasted_iota(jnp.int32, (c_len, dh), 1).astype(f32)
    rel = ri - ci
    decay = (jnp.where(rel >= 0, jnp.exp(lgf * jnp.maximum(rel, 0.0)), 0.0)
             + jnp.where(rel <= 0, jnp.exp(lgb * jnp.maximum(-rel, 0.0)), 0.0))
    kw_f = jnp.exp(lgf * (c_len - 1 - ri))
    kw_b = jnp.exp(lgb * ri)
    qw_f = jnp.exp(lgf * (ri + 1.0))
    qw_b = jnp.exp(lgb * (c_len - ri))
    cd_f = jnp.exp(jnp.full((dh, dh), lgf * c_len, f32))
    cd_b = jnp.exp(jnp.full((dh, dh), lgb * c_len, f32))

    def rows_of(c):
        return pl.ds(pl.multiple_of(c * c_len, c_len), c_len)

    def kt_dot(kb, vw):
        return lax.dot_general(kb, vw, (((0,), (0,)), ((), ())), preferred_element_type=f32)

    def prep(c, carry):
        rows = rows_of(c)
        q, k = q_ref[rows, :], k_ref[rows, :]
        qf, kf = q.astype(f32), k.astype(f32)
        if rope:
            cos, sin = cos_ref[rows, :], sin_ref[rows, :]
            qf = qf * cos + jnp.dot(q, perm_ref[...], preferred_element_type=f32) * sin
            kf = kf * cos + jnp.dot(k, perm_ref[...], preferred_element_type=f32) * sin
        kb = (kf * scale).astype(bf16)
        qs[rows, :] = qf.astype(bf16)
        ks[rows, :] = kb
        v = v_ref[rows, :].astype(f32)
        vw = jnp.concatenate([v * kw_f, v * kw_b], axis=1).astype(bf16)
        ds = kt_dot(kb, vw)
        sf[c] = ds[:, :dh]
        sb[c] = ds[:, dh:]
        return carry

    unroll = min(n_chunks, CHUNK_UNROLL)
    lax.fori_loop(0, n_chunks, prep, 0, unroll=unroll)

    if ctx_len:
        mi = lax.broadcasted_iota(jnp.int32, (ctx_len, dh), 0).astype(f32)
        kcb = (kc_ref[...].astype(f32) * scale).astype(bf16)
        vc = vc_ref[...].astype(f32)
        vcw = jnp.concatenate([vc * jnp.exp(lgf * (ctx_len - 1 - mi)), vc * jnp.exp(lgb * mi)],
                              axis=1).astype(bf16)
        s0 = kt_dot(kcb, vcw)
        s0_f, s0_b = s0[:, :dh], s0[:, dh:]
    else:
        s0_f = s0_b = jnp.zeros((dh, dh), f32)

    def scan_f(c, s):
        nxt = cd_f * s + sf[c]
        sf[c] = s
        return nxt

    def scan_b(j, s):
        c = n_chunks - 1 - j
        nxt = cd_b * s + sb[c]
        sb[c] = s
        return nxt

    lax.fori_loop(0, n_chunks, scan_f, s0_f)
    lax.fori_loop(0, n_chunks, scan_b, s0_b)

    def emit(c, carry):
        rows = rows_of(c)
        q, k, v = qs[rows, :], ks[rows, :], v_ref[rows, :]
        sc = lax.dot_general(q, k, (((1,), (1,)), ((), ())), preferred_element_type=f32) * decay
        inner = jnp.dot(sc.astype(bf16), v, preferred_element_type=f32)
        s2 = jnp.concatenate([sf[c], sb[c]], axis=1).astype(bf16)
        cr = jnp.dot(q, s2, preferred_element_type=f32)
        ret = inner + qw_f * cr[:, :dh] + qw_b * cr[:, dh:]
        g = g_ref[rows, :].astype(f32)
        o_ref[rows, :] = (jax.nn.silu(g) * _rms(ret)).astype(bf16)
        return carry

    lax.fori_loop(0, n_chunks, emit, 0, unroll=unroll)


def _retention(proj, log_decay, rows, rope_tabs, prev_out):
    latent = prev_out is None
    seq = rows.seq if latent else rows.ctx_len
    n_chunks = seq // CHUNK
    row0 = 0 if latent else rows.m_x // seq
    hq = MIX_W // GROUP_DIM

    def col(off):
        return pl.BlockSpec((seq, GROUP_DIM), lambda b, h: (row0 + b, off * hq + h))

    in_specs = [pl.BlockSpec(memory_space=pltpu.SMEM), col(0), col(1), col(2), col(3)]
    args = [log_decay, proj, proj, proj, proj]
    if latent:
        cb0 = rows.m_x // rows.ctx_len
        for off in (1, 2):
            in_specs.append(pl.BlockSpec((rows.ctx_len, GROUP_DIM),
                                         lambda b, h, off=off: (cb0 + b, off * hq + h)))
            args.append(proj)
        cos, sin, perm = rope_tabs
        in_specs += [pl.BlockSpec((seq, GROUP_DIM), lambda b, h: (0, 0)),
                     pl.BlockSpec((seq, GROUP_DIM), lambda b, h: (0, 0)),
                     pl.BlockSpec((GROUP_DIM, GROUP_DIM), lambda b, h: (0, 0))]
        args += [cos, sin, perm]
        aliases = {}
    else:
        in_specs.append(pl.BlockSpec(memory_space=pl.ANY))
        args.append(prev_out)
        aliases = {len(args) - 1: 0}

    kernel = functools.partial(_retention_kernel, n_chunks=n_chunks, rope=latent,
                               ctx_len=rows.ctx_len if latent else 0)
    if not latent:
        def kernel(*refs, _k=kernel):
            return _k(*refs[:5], *refs[6:])

    return pl.pallas_call(
        kernel,
        grid=(rows.batch, hq),
        in_specs=in_specs,
        out_specs=pl.BlockSpec((seq, GROUP_DIM), lambda b, h: (row0 + b, h)),
        out_shape=jax.ShapeDtypeStruct((rows.m, MIX_W), bf16),
        scratch_shapes=[pltpu.VMEM((seq, GROUP_DIM), bf16), pltpu.VMEM((seq, GROUP_DIM), bf16),
                        pltpu.VMEM((n_chunks, GROUP_DIM, GROUP_DIM), f32),
                        pltpu.VMEM((n_chunks, GROUP_DIM, GROUP_DIM), f32)],
        input_output_aliases=aliases,
        compiler_params=_cparams("parallel", "parallel"),
        name="retention_x" if latent else "retention_ctx",
    )(*args)


FOURIER_TILE = 512
FOURIER_EXTRA = 16


def _dft_tables(seq):
    def phase(k, n):
        ang = (k % n).astype(f32) * (2.0 * jnp.pi / n)
        return jnp.cos(ang), jnp.sin(ang)

    def cs(n_f, n):
        r = 1 << ((n_f.bit_length() - 1) // 2)
        t = jnp.arange(n, dtype=jnp.int32)
        ca, sa = phase((r * jnp.arange(n_f // r, dtype=jnp.int32))[:, None] * t[None, :], n)
        cb, sb = phase(jnp.arange(r, dtype=jnp.int32)[:, None] * t[None, :], n)
        cos = ca[:, None, :] * cb[None] - sa[:, None, :] * sb[None]
        sin = sa[:, None, :] * cb[None] + ca[:, None, :] * sb[None]
        return cos.reshape(n_f, n), sin.reshape(n_f, n)

    half = seq // 2
    tm = _tile(half, FOURIER_TILE)
    n_tiles = half // tm
    cl, sl = cs(half, seq)
    nyq = jnp.where(jnp.arange(seq) % 2 == 0, 1.0, -1.0).astype(f32)
    extra = jnp.zeros((n_tiles, FOURIER_EXTRA, seq), f32).at[:, 0, :].set(nyq)
    pos = jnp.concatenate([cl.reshape(n_tiles, tm, seq), sl.reshape(n_tiles, tm, seq), extra], axis=1)

    cc, sc = cs(GROUP_DIM, GROUP_DIM)
    norm = (seq * GROUP_DIM) ** -0.5
    mix = jnp.concatenate([jnp.concatenate([cc, cc], 1), jnp.concatenate([-sc, sc], 1)], 0) * norm

    s = jnp.arange(tm)
    flip = ((s[:, None] + s[None, :] == tm) & (s[:, None] >= 1)).astype(bf16)
    return pos.astype(bf16), mix.astype(bf16), flip


def _fourier_kernel(z_ref, pos_ref, mix_ref, flip_ref, o_ref, stash, *, seq, tm, n_tiles):
    j = pl.program_id(1)
    m = n_tiles - 1 - j

    def mixed(cz, sz):
        qs, ps = [], []
        for g in range(N_GROUPS):
            cols = slice(g * GROUP_DIM, (g + 1) * GROUP_DIM)
            szg = jnp.zeros_like(cz[:, cols]) if sz is None else sz[:, cols]
            lhs = jnp.concatenate([cz[:, cols], szg], axis=1).astype(bf16)
            out = jnp.dot(lhs, mix_ref[...], preferred_element_type=f32)
            qs.append(out[:, :GROUP_DIM])
            ps.append(out[:, GROUP_DIM:])
        return jnp.concatenate(qs, axis=1), jnp.concatenate(ps, axis=1)

    r = jnp.dot(pos_ref[...], z_ref[...], preferred_element_type=f32)
    q, p = mixed(r[:tm], r[tm:2 * tm])

    @pl.when(j == 0)
    def _():
        stash[...] = mixed(r[2 * tm:], None)[1]

    above = stash[0:1, :]
    o_ref[pl.ds(pl.multiple_of(m * tm, tm), tm), :] = q.astype(bf16)
    mir = jnp.dot(flip_ref[...], p.astype(bf16), preferred_element_type=f32)
    row = lax.broadcasted_iota(jnp.int32, mir.shape, 0)
    mir = jnp.where(row == 0, above, mir)
    o_ref[pl.ds(pl.multiple_of(seq // 2 + j * tm, tm), tm), :] = mir.astype(bf16)
    stash[...] = p[:FOURIER_EXTRA]


def _fourier(proj, rows, tables, prev_out):
    latent = prev_out is None
    seq = rows.seq if latent else rows.ctx_len
    pos_tab, mix_tab, flip_tab = tables
    n_tiles, tab_rows, _ = pos_tab.shape
    tm = flip_tab.shape[0]
    row0 = 0 if latent else rows.m_x // seq
    in_specs = [pl.BlockSpec((seq, MIX_W), lambda b, j: (row0 + b, COL_F)),
                pl.BlockSpec((None, tab_rows, seq), lambda b, j: (n_tiles - 1 - j, 0, 0)),
                pl.BlockSpec((2 * GROUP_DIM, 2 * GROUP_DIM), lambda b, j: (0, 0)),
                pl.BlockSpec((tm, tm), lambda b, j: (0, 0))]
    args = [proj, pos_tab, mix_tab, flip_tab]
    kernel = functools.partial(_fourier_kernel, seq=seq, tm=tm, n_tiles=n_tiles)
    aliases = {}
    if not latent:
        in_specs.append(pl.BlockSpec(memory_space=pl.ANY))
        args.append(prev_out)
        aliases = {4: 0}

        def kernel(z_ref, pos_ref, mix_ref, flip_ref, _carrier, o_ref, stash, _k=kernel):
            return _k(z_ref, pos_ref, mix_ref, flip_ref, o_ref, stash)

    return pl.pallas_call(
        kernel,
        grid=(rows.batch, n_tiles),
        in_specs=in_specs,
        out_specs=pl.BlockSpec((seq, MIX_W), lambda b, j: (row0 + b, 0)),
        out_shape=jax.ShapeDtypeStruct((rows.m, MIX_W), bf16),
        scratch_shapes=[pltpu.VMEM((FOURIER_EXTRA, MIX_W), f32)],
        input_output_aliases=aliases,
        compiler_params=_cparams("parallel", "arbitrary"),
        name="fourier_x" if latent else "fourier_ctx",
    )(*args)


def _sgu_kernel(u_ref, v_ref, w_ref, b_ref, o_ref, *, tm):
    for ch in range(tm // CHUNK):
        rows = slice(ch * CHUNK, (ch + 1) * CHUNK)
        u = jax.nn.gelu(u_ref[rows, :].astype(f32))
        v = jax.nn.gelu(v_ref[rows, :].astype(f32))
        outs = []
        for g in range(N_GROUPS):
            cols = slice(g * GROUP_DIM, (g + 1) * GROUP_DIM)
            vg = v[:, cols]
            dv = vg - jnp.mean(vg, axis=-1, keepdims=True)
            vn = dv * lax.rsqrt(jnp.mean(dv * dv, axis=-1, keepdims=True) + EPS)
            s = jnp.dot(w_ref[g], vn.astype(bf16), preferred_element_type=f32) + b_ref[:, cols]
            outs.append(u[:, cols] * s)
        o_ref[rows, :] = jnp.concatenate(outs, axis=1).astype(bf16)


def _conv_kernel(b_ref, c_ref, x_ref, w_ref, o_ref, *, tm, n_x_tiles, ctx_len):
    y = c_ref[...].astype(f32) * x_ref[...].astype(f32)
    r = lax.broadcasted_iota(jnp.int32, y.shape, 0)
    period = jnp.where(pl.program_id(0) < n_x_tiles, GRID_W, ctx_len)
    rp = r & (period - 1)
    prev = jnp.where(rp == 0, 0.0, pltpu.roll(y, 1, axis=0))
    nxt = jnp.where(rp == period - 1, 0.0, pltpu.roll(y, tm - 1, axis=0))
    conv = w_ref[0:1, :] * prev + w_ref[1:2, :] * y + w_ref[2:3, :] * nxt
    o_ref[...] = (b_ref[...].astype(f32) * conv).astype(bf16)


def _local_mix_kernel(u_ref, v_ref, ws_ref, bs_ref, b_ref, c_ref, x_ref, wc_ref, oc_ref, od_ref,
                      *, tm, n_x_tiles, ctx_len):
    _sgu_kernel(u_ref, v_ref, ws_ref, bs_ref, oc_ref, tm=tm)
    _conv_kernel(b_ref, c_ref, x_ref, wc_ref, od_ref, tm=tm, n_x_tiles=n_x_tiles, ctx_len=ctx_len)


def _local_mix(proj, sgu_w, layer, sgu_bias, conv_w, rows, tm, n_rows):
    assert GRID_W & (GRID_W - 1) == 0 and rows.ctx_len & (rows.ctx_len - 1) == 0
    assert tm % GRID_W == 0 and tm % rows.ctx_len == 0 and tm % CHUNK == 0

    def col(c):
        return pl.BlockSpec((tm, MIX_W), lambda i: (i, c))

    out = jax.ShapeDtypeStruct((rows.m, MIX_W), bf16)
    return pl.pallas_call(
        functools.partial(_local_mix_kernel, tm=tm, n_x_tiles=rows.m_x // tm, ctx_len=rows.ctx_len),
        grid=(n_rows // tm,),
        in_specs=[col(COL_U), col(COL_V),
                  pl.BlockSpec((None, N_GROUPS, CHUNK, CHUNK), lambda i: (layer, 0, 0, 0)),
                  pl.BlockSpec((CHUNK, MIX_W), lambda i: (0, 0)),
                  col(COL_SB), col(COL_SC), col(COL_SX),
                  pl.BlockSpec((None, 3, MIX_W), lambda i: (layer, 0, 0))],
        out_specs=[col(0), col(0)],
        out_shape=[out, out],
        compiler_params=_cparams("parallel"),
        name="local_mix",
    )(proj, proj, sgu_w, sgu_bias, proj, proj, proj, conv_w)


def _merge_kernel(hx_ref, ya_ref, yb_ref, yc_ref, yd_ref, wg_ref, wb_ref, o_ref, acc, *, row_chunk):
    n = pl.program_id(2)
    for k, y_ref in enumerate((ya_ref, yb_ref, yc_ref, yd_ref)):
        @pl.when(n == k)
        def _(k=k, y_ref=y_ref):
            for r in range(hx_ref.shape[0] // row_chunk):
                rows = slice(r * row_chunk, (r + 1) * row_chunk)
                gate = jax.nn.sigmoid(jnp.dot(hx_ref[rows, :], wg_ref[...], preferred_element_type=f32))
                term = gate * jnp.dot(y_ref[rows, :], wb_ref[...], preferred_element_type=f32)
                if k == 0:
                    acc[rows, :] = term
                elif k < N_BRANCH - 1:
                    acc[rows, :] += term
                else:
                    o_ref[rows, :] = (acc[rows, :] + term).astype(bf16)


def _merge(hx, ys, w_in, w_branch, layer, tm, tn, n_rows):
    d = hx.shape[1]
    m = n_rows
    nj = d // tn
    gate0 = N_MIX_COLS // tn
    y_spec = pl.BlockSpec((tm, MIX_W), lambda i, j, n: (i, 0))
    return pl.pallas_call(
        functools.partial(_merge_kernel, row_chunk=_tile(tm, MERGE_ROW_CHUNK)),
        grid=(m // tm, nj, N_BRANCH),
        in_specs=[pl.BlockSpec((tm, d), lambda i, j, n: (i, 0)), y_spec, y_spec, y_spec, y_spec,
                  pl.BlockSpec((None, d, tn), lambda i, j, n: (layer, 0, gate0 + n * nj + j)),
                  pl.BlockSpec((None, None, MIX_W, tn), lambda i, j, n: (layer, n, 0, j))],
        out_specs=pl.BlockSpec((tm, tn), lambda i, j, n: (i, j)),
        out_shape=jax.ShapeDtypeStruct((m, d), bf16),
        scratch_shapes=[pltpu.VMEM((tm, tn), f32)],
        compiler_params=_cparams("parallel", "arbitrary", "arbitrary"),
        name="merge",
    )(hx, *ys, w_in, w_branch)


def _outproj_kernel(m_ref, w_ref, x_ref, g1_ref, g2_ref, gate_ref, sh_ref, sc_ref, xo_ref, ho_ref, *, row_chunk):
    res_scale = gate_ref[...] * g1_ref[...]
    mod_scale = g2_ref[...] * (1.0 + sc_ref[...])
    for r in range(m_ref.shape[0] // row_chunk):
        rows = slice(r * row_chunk, (r + 1) * row_chunk)
        mix = jnp.dot(m_ref[rows, :], w_ref[...], preferred_element_type=f32)
        xn = x_ref[rows, :] + _rms(mix) * res_scale
        xo_ref[rows, :] = xn
        ho_ref[rows, :] = (_rms(xn) * mod_scale + sh_ref[...]).astype(bf16)


def _outproj(merged, w_out, xs, g1, g2, mods, layer, rows, n_rows):
    d = xs.shape[1]
    tm = rows.tm
    row = pl.BlockSpec((tm, d), lambda i: (i, 0))
    return pl.pallas_call(
        functools.partial(_outproj_kernel, row_chunk=_tile(tm, OUTPROJ_ROW_CHUNK)),
        grid=(n_rows // tm,),
        in_specs=[row, pl.BlockSpec((None, d, d), lambda i: (layer, 0, 0)), row, _vec_spec(d), _vec_spec(d),
                  rows.mod_spec(layer, 2, d), rows.mod_spec(layer, 3, d), rows.mod_spec(layer, 4, d)],
        out_specs=[row, row],
        out_shape=[jax.ShapeDtypeStruct((n_rows, d), f32), jax.ShapeDtypeStruct((n_rows, d), bf16)],
        compiler_params=_cparams("parallel"),
        name="outproj",
    )(merged, w_out, xs, g1, g2, mods, mods, mods)


def _ffn_kernel(*refs, n_f, has_next, row_chunk):
    if has_next:
        h_ref, wa_ref, wb_ref, w2_ref, x_ref, g3_ref, gate_ref, gn_ref, sh_ref, sc_ref, xo_ref, ho_ref = refs
    else:
        h_ref, wa_ref, wb_ref, w2_ref, x_ref, g3_ref, gate_ref, xo_ref = refs
    f = pl.program_id(1)

    def step(first, last):
        if last:
            res_scale = gate_ref[...] * g3_ref[...]
            if has_next:
                mod_scale = gn_ref[...] * (1.0 + sc_ref[...])
        for r in range(h_ref.shape[0] // row_chunk):
            rows = slice(r * row_chunk, (r + 1) * row_chunk)
            h = h_ref[rows, :]
            a = jnp.dot(h, wa_ref[...], preferred_element_type=f32)
            b = jnp.dot(h, wb_ref[...], preferred_element_type=f32)
            acc = jnp.dot((jax.nn.silu(a) * b).astype(bf16), w2_ref[...], preferred_element_type=f32)
            if not first:
                acc = xo_ref[rows, :] + acc
            if last:
                acc = x_ref[rows, :] + _rms(acc) * res_scale
                if has_next:
                    ho_ref[rows, :] = (_rms(acc) * mod_scale + sh_ref[...]).astype(bf16)
            xo_ref[rows, :] = acc

    if n_f == 1:
        step(True, True)
    else:
        pl.when(f == 0)(lambda: step(True, False))
        pl.when(f == n_f - 1)(lambda: step(False, True))
        if n_f > 2:
            pl.when((f > 0) & (f < n_f - 1))(lambda: step(False, False))


def _ffn(hx, w1, w2, xs, g3, g_next, mods, layer, rows, tf, n_rows):
    d = xs.shape[1]
    n_f = w2.shape[1] // tf
    tm = rows.tm
    has_next = g_next is not None
    row = pl.BlockSpec((tm, d), lambda i, f: (i, 0))
    in_specs = [row,
                pl.BlockSpec((None, d, tf), lambda i, f: (layer, 0, f)),
                pl.BlockSpec((None, d, tf), lambda i, f: (layer, 0, n_f + f)),
                pl.BlockSpec((None, tf, d), lambda i, f: (layer, f, 0)),
                row, _vec_spec(d), rows.mod_spec(layer, 5, d)]
    args = [hx, w1, w1, w2, xs, g3, mods]
    out_specs = [row]
    out_shape = [jax.ShapeDtypeStruct((n_rows, d), f32)]
    if has_next:
        in_specs += [_vec_spec(d), rows.mod_spec(layer + 1, 0, d), rows.mod_spec(layer + 1, 1, d)]
        args += [g_next, mods, mods]
        out_specs.append(row)
        out_shape.append(jax.ShapeDtypeStruct((n_rows, d), bf16))
    out = pl.pallas_call(
        functools.partial(_ffn_kernel, n_f=n_f, has_next=has_next, row_chunk=_tile(tm, FFN_ROW_CHUNK)),
        grid=(n_rows // tm, n_f),
        in_specs=in_specs,
        out_specs=out_specs,
        out_shape=out_shape,
        compiler_params=_cparams("parallel", "arbitrary"),
        name="ffn",
    )(*args)
    return out if has_next else (out[0], None)


def kernel(x, c, ctx, c_ctx, ada_w, ada_b, norm_g, w_in, ret_log_decay, conv_w, sgu_w, sgu_b,
           w_branch, w_out, ffn_w_in, ffn_w_out):
    batch, seq, d = x.shape
    ctx_len = ctx.shape[1]
    depth = ada_w.shape[0]
    d_ff = ffn_w_out.shape[1]
    assert batch + 1 <= MOD_ROWS and w_in.shape[2] == N_MIX_COLS + N_BRANCH * d

    rows = _Rows(batch, seq, ctx_len, _tile(batch * ctx_len, 512))
    tm_big = _tile(batch * ctx_len, 1024)

    cc = jnp.zeros((MOD_ROWS, d), f32).at[:batch].set(c).at[batch].set(c_ctx)
    mods = _adaln(cc, ada_w, ada_b).reshape(depth * MOD_ROWS * 6, 1, d)

    rope_tabs = _rope_tables(seq)
    dft_x = _dft_tables(seq)
    dft_c = _dft_tables(ctx_len)

    xs, hx = _norm_mod(x.reshape(batch * seq, d), ctx.reshape(batch * ctx_len, d),
                       norm_g[0, 0:1], mods, rows)

    w_in_b, w_branch_b, w_out_b = w_in.astype(bf16), w_branch.astype(bf16), w_out.astype(bf16)
    ffn_w_in_b, ffn_w_out_b, sgu_w_b = ffn_w_in.astype(bf16), ffn_w_out.astype(bf16), sgu_w.astype(bf16)
    tn = _tile(d, 1024)
    assert N_MIX_COLS % tn == 0

    for l in range(depth):
        sgu_bias = jnp.repeat(jnp.transpose(sgu_b[l]), GROUP_DIM, axis=1)

        last = l + 1 == depth
        n_rows = rows.m_x if last else rows.m

        proj = _matmul(hx, w_in_b, l, N_MIX_COLS, tm_big, N_MIX_COLS // 2, "inproj")
        ya = _retention(proj, ret_log_decay[l], rows, rope_tabs, None)
        yb = _fourier(proj, rows, dft_x, None)
        if not last:
            ya = _retention(proj, ret_log_decay[l], rows, None, ya)
            yb = _fourier(proj, rows, dft_c, yb)
        yc, yd = _local_mix(proj, sgu_w_b, l, sgu_bias, conv_w, rows, rows.tm, n_rows)
        merged = _merge(hx, (ya, yb, yc, yd), w_in_b, w_branch_b, l, tm_big, tn, n_rows)
        xs, hx2 = _outproj(merged, w_out_b, xs, norm_g[l, 1:2], norm_g[l, 2:3], mods, l, rows, n_rows)
        g_next = None if last else norm_g[l + 1, 0:1]
        xs, hx = _ffn(hx2, ffn_w_in_b, ffn_w_out_b, xs, norm_g[l, 3:4], g_next, mods, l, rows,
                      _tile(d_ff, 512), n_rows)

    return xs.reshape(batch, seq, d)
```

```python
import functools

import jax
import jax.numpy as jnp
from jax import lax
from jax.experimental import pallas as pl
from jax.experimental.pallas import tpu as pltpu

f32 = jnp.float32
bf16 = jnp.bfloat16

GRID_W = 64
MIX_W = 512
N_BRANCH = 4
GROUP_DIM = 128
N_GROUPS = MIX_W // GROUP_DIM
CHUNK = 128
ROPE_BASE = 10000.0
EPS = 1e-6
N_MIX_COLS = 10 * MIX_W
COL_F, COL_U, COL_V, COL_SB, COL_SC, COL_SX = 4, 5, 6, 7, 8, 9
MOD_ROWS = 16
CHUNK_UNROLL = 16
FFN_ROW_CHUNK = 256
MERGE_ROW_CHUNK = 512
OUTPROJ_ROW_CHUNK = 256

VMEM_LIMIT_BYTES = 56 * 1024 * 1024


def _cparams(*sem):
    return pltpu.CompilerParams(dimension_semantics=sem, vmem_limit_bytes=VMEM_LIMIT_BYTES)


def _tile(n, pref):
    t = min(n, pref)
    while n % t:
        t //= 2
    return t


def _rms(x):
    return x * lax.rsqrt(jnp.mean(x * x, axis=-1, keepdims=True) + EPS)


def _adaln_kernel(c_ref, w_ref, b_ref, o_ref):
    s = jax.nn.silu(c_ref[...]).astype(bf16)
    o_ref[...] = jnp.dot(s, w_ref[...].astype(bf16), preferred_element_type=f32) + b_ref[...]


def _adaln(cc, ada_w, ada_b):
    depth, d, n6 = ada_w.shape
    tn = _tile(n6, 1024)
    return pl.pallas_call(
        _adaln_kernel,
        grid=(depth, n6 // tn),
        in_specs=[
            pl.BlockSpec((MOD_ROWS, d), lambda l, j: (0, 0)),
            pl.BlockSpec((None, d, tn), lambda l, j: (l, 0, j)),
            pl.BlockSpec((None, 1, tn), lambda l, j: (l, 0, j)),
        ],
        out_specs=pl.BlockSpec((None, MOD_ROWS, tn), lambda l, j: (l, 0, j)),
        out_shape=jax.ShapeDtypeStruct((depth, MOD_ROWS, n6), f32),
        compiler_params=_cparams("parallel", "parallel"),
        name="adaln",
    )(cc, ada_w, ada_b.reshape(depth, 1, n6))


class _Rows:
    def __init__(self, batch, seq, ctx_len, tm):
        self.batch, self.seq, self.ctx_len, self.tm = batch, seq, ctx_len, tm
        self.m_x, self.m_c = batch * seq, batch * ctx_len
        self.m = self.m_x + self.m_c
        assert seq % tm == 0 and self.m_c % tm == 0
        self.n_x_tiles = self.m_x // tm
        self.n_tiles = self.m // tm
        self.per_batch = seq // tm

    def mod_spec(self, layer, k, d, extra_axes=0):
        def index(i, *_):
            r = jnp.where(i < self.n_x_tiles, i // self.per_batch, self.batch)
            return ((layer * MOD_ROWS + r) * 6 + k, 0, 0)
        return pl.BlockSpec((None, 1, d), index)


def _vec_spec(d):
    return pl.BlockSpec((1, d), lambda i, *_: (0, 0))


def _norm_mod_kernel(x_ref, c_ref, g_ref, sh_ref, sc_ref, xs_ref, hx_ref, *, n_x_tiles):
    def emit(x):
        xs_ref[...] = x
        hx_ref[...] = (_rms(x) * g_ref[...] * (1.0 + sc_ref[...]) + sh_ref[...]).astype(bf16)

    @pl.when(pl.program_id(0) < n_x_tiles)
    def _():
        emit(x_ref[...])

    @pl.when(pl.program_id(0) >= n_x_tiles)
    def _():
        emit(c_ref[...])


def _norm_mod(x2d, c2d, g, mods, rows):
    d = x2d.shape[1]
    tm, nxt = rows.tm, rows.n_x_tiles
    return pl.pallas_call(
        functools.partial(_norm_mod_kernel, n_x_tiles=nxt),
        grid=(rows.n_tiles,),
        in_specs=[
            pl.BlockSpec((tm, d), lambda i: (jnp.minimum(i, nxt - 1), 0)),
            pl.BlockSpec((tm, d), lambda i: (jnp.maximum(i - nxt, 0), 0)),
            _vec_spec(d),
            rows.mod_spec(0, 0, d),
            rows.mod_spec(0, 1, d),
        ],
        out_specs=[pl.BlockSpec((tm, d), lambda i: (i, 0)), pl.BlockSpec((tm, d), lambda i: (i, 0))],
        out_shape=[jax.ShapeDtypeStruct((rows.m, d), f32), jax.ShapeDtypeStruct((rows.m, d), bf16)],
        compiler_params=_cparams("parallel"),
        name="norm_mod",
    )(x2d, c2d, g, mods, mods)


def _matmul_kernel(a_ref, b_ref, o_ref):
    o_ref[...] = jnp.dot(a_ref[...], b_ref[...], preferred_element_type=f32).astype(o_ref.dtype)


def _matmul(a, b_stack, layer, n, tm, tn, name):
    m, k = a.shape
    return pl.pallas_call(
        _matmul_kernel,
        grid=(m // tm, n // tn),
        in_specs=[pl.BlockSpec((tm, k), lambda i, j: (i, 0)),
                  pl.BlockSpec((None, k, tn), lambda i, j: (layer, 0, j))],
        out_specs=pl.BlockSpec((tm, tn), lambda i, j: (i, j)),
        out_shape=jax.ShapeDtypeStruct((m, n), bf16),
        compiler_params=_cparams("parallel", "arbitrary"),
        name=name,
    )(a, b_stack)


def _rope_tables(seq):
    t = jnp.arange(seq)
    n = GROUP_DIM // 2
    freqs = ROPE_BASE ** (-jnp.arange(0, n, 2, dtype=f32) / n)

    def tab(pos):
        ang = pos.astype(f32)[:, None] * freqs[None, :]
        c, s = jnp.cos(ang), jnp.sin(ang)
        return jnp.concatenate([c, c], -1), jnp.concatenate([-s, s], -1)

    c1, s1 = tab(t // GRID_W)
    c2, s2 = tab(t % GRID_W)
    dst = jnp.arange(GROUP_DIM)
    src = jnp.where((dst % n) < n // 2, dst + n // 2, dst - n // 2)
    perm = (jnp.arange(GROUP_DIM)[:, None] == src[None, :]).astype(bf16)
    return jnp.concatenate([c1, c2], -1), jnp.concatenate([s1, s2], -1), perm


def _retention_kernel(*refs, n_chunks, rope, ctx_len):
    refs = list(refs)
    lg_ref, q_ref, k_ref, v_ref, g_ref = refs[:5]
    pos = 5
    if ctx_len:
        kc_ref, vc_ref = refs[pos:pos + 2]
        pos += 2
    if rope:
        cos_ref, sin_ref, perm_ref = refs[pos:pos + 3]
        pos += 3
    o_ref, qs, ks, sf, sb = refs[pos:]

    c_len, dh = CHUNK, GROUP_DIM
    h = pl.program_id(1)
    lgf, lgb = lg_ref[0, h], lg_ref[1, h]
    scale = dh ** -0.5
    ri = lax.broadcasted_iota(jnp.int32, (c_len, dh), 0).astype(f32)
    ci = lax.broadcasted_iota(jnp.int32, (c_len, dh), 1).astype(f32)
    rel = ri - ci
    decay = (jnp.where(rel >= 0, jnp.exp(lgf * jnp.maximum(rel, 0.0)), 0.0)
             + jnp.where(rel <= 0, jnp.exp(lgb * jnp.maximum(-rel, 0.0)), 0.0))
    kw_f = jnp.exp(lgf * (c_len - 1 - ri))
    kw_b = jnp.exp(lgb * ri)
    qw_f = jnp.exp(lgf * (ri + 1.0))
    qw_b = jnp.exp(lgb * (c_len - ri))
    cd_f = jnp.exp(jnp.full((dh, dh), lgf * c_len, f32))
    cd_b = jnp.exp(jnp.full((dh, dh), lgb * c_len, f32))

    def rows_of(c):
        return pl.ds(pl.multiple_of(c * c_len, c_len), c_len)

    def kt_dot(kb, vw):
        return lax.dot_general(kb, vw, (((0,), (0,)), ((), ())), preferred_element_type=f32)

    def prep(c, carry):
        rows = rows_of(c)
        q, k = q_ref[rows, :], k_ref[rows, :]
        qf, kf = q.astype(f32), k.astype(f32)
        if rope:
            cos, sin = cos_ref[rows, :], sin_ref[rows, :]
            qf = qf * cos + jnp.dot(q, perm_ref[...], preferred_element_type=f32) * sin
            kf = kf * cos + jnp.dot(k, perm_ref[...], preferred_element_type=f32) * sin
        kb = (kf * scale).astype(bf16)
        qs[rows, :] = qf.astype(bf16)
        ks[rows, :] = kb
        v = v_ref[rows, :].astype(f32)
        vw = jnp.concatenate([v * kw_f, v * kw_b], axis=1).astype(bf16)
        ds = kt_dot(kb, vw)
        sf[c] = ds[:, :dh]
        sb[c] = ds[:, dh:]
        return carry

    unroll = min(n_chunks, CHUNK_UNROLL)
    lax.fori_loop(0, n_chunks, prep, 0, unroll=unroll)

    if ctx_len:
        mi = lax.broadcasted_iota(jnp.int32, (ctx_len, dh), 0).astype(f32)
        kcb = (kc_ref[...].astype(f32) * scale).astype(bf16)
        vc = vc_ref[...].astype(f32)
        vcw = jnp.concatenate([vc * jnp.exp(lgf * (ctx_len - 1 - mi)), vc * jnp.exp(lgb * mi)],
                              axis=1).astype(bf16)
        s0 = kt_dot(kcb, vcw)
        s0_f, s0_b = s0[:, :dh], s0[:, dh:]
    else:
        s0_f = s0_b = jnp.zeros((dh, dh), f32)

    def scan_f(c, s):
        nxt = cd_f * s + sf[c]
        sf[c] = s
        return nxt

    def scan_b(j, s):
        c = n_chunks - 1 - j
        nxt = cd_b * s + sb[c]
        sb[c] = s
        return nxt

    lax.fori_loop(0, n_chunks, scan_f, s0_f)
    lax.fori_loop(0, n_chunks, scan_b, s0_b)

    def emit(c, carry):
        rows = rows_of(c)
        q, k, v = qs[rows, :], ks[rows, :], v_ref[rows, :]
        sc = lax.dot_general(q, k, (((1,), (1,)), ((), ())), preferred_element_type=f32) * decay
        inner = jnp.dot(sc.astype(bf16), v, preferred_element_type=f32)
        s2 = jnp.concatenate([sf[c], sb[c]], axis=1).astype(bf16)
        cr = jnp.dot(q, s2, preferred_element_type=f32)
        ret = inner + qw_f * cr[:, :dh] + qw_b * cr[:, dh:]
        g = g_ref[rows, :].astype(f32)
        o_ref[rows, :] = (jax.nn.silu(g) * _rms(ret)).astype(bf16)
        return carry

    lax.fori_loop(0, n_chunks, emit, 0, unroll=unroll)


def _retention(proj, log_decay, rows, rope_tabs, latent):
    seq = rows.seq if latent else rows.ctx_len
    n_chunks = seq // CHUNK
    row0 = 0 if latent else rows.m_x // seq
    hq = MIX_W // GROUP_DIM

    def col(off):
        return pl.BlockSpec((seq, GROUP_DIM), lambda b, h: (row0 + b, off * hq + h))

    in_specs = [pl.BlockSpec(memory_space=pltpu.SMEM), col(0), col(1), col(2), col(3)]
    args = [log_decay, proj, proj, proj, proj]
    if latent:
        cb0 = rows.m_x // rows.ctx_len
        for off in (1, 2):
            in_specs.append(pl.BlockSpec((rows.ctx_len, GROUP_DIM),
                                         lambda b, h, off=off: (cb0 + b, off * hq + h)))
            args.append(proj)
        cos, sin, perm = rope_tabs
        in_specs += [pl.BlockSpec((seq, GROUP_DIM), lambda b, h: (0, 0)),
                     pl.BlockSpec((seq, GROUP_DIM), lambda b, h: (0, 0)),
                     pl.BlockSpec((GROUP_DIM, GROUP_DIM), lambda b, h: (0, 0))]
        args += [cos, sin, perm]

    return pl.pallas_call(
        functools.partial(_retention_kernel, n_chunks=n_chunks, rope=latent,
                          ctx_len=rows.ctx_len if latent else 0),
        grid=(rows.batch, hq),
        in_specs=in_specs,
        out_specs=pl.BlockSpec((seq, GROUP_DIM), lambda b, h: (b, h)),
        out_shape=jax.ShapeDtypeStruct((rows.batch * seq, MIX_W), bf16),
        scratch_shapes=[pltpu.VMEM((seq, GROUP_DIM), bf16), pltpu.VMEM((seq, GROUP_DIM), bf16),
                        pltpu.VMEM((n_chunks, GROUP_DIM, GROUP_DIM), f32),
                        pltpu.VMEM((n_chunks, GROUP_DIM, GROUP_DIM), f32)],
        compiler_params=_cparams("parallel", "parallel"),
        name="retention_x" if latent else "retention_ctx",
    )(*args)


FOURIER_TILE = 512
FOURIER_EXTRA = 16


def _dft_tables(seq):
    def phase(k, n):
        ang = (k % n).astype(f32) * (2.0 * jnp.pi / n)
        return jnp.cos(ang), jnp.sin(ang)

    def cs(n_f, n):
        r = 1 << ((n_f.bit_length() - 1) // 2)
        t = jnp.arange(n, dtype=jnp.int32)
        ca, sa = phase((r * jnp.arange(n_f // r, dtype=jnp.int32))[:, None] * t[None, :], n)
        cb, sb = phase(jnp.arange(r, dtype=jnp.int32)[:, None] * t[None, :], n)
        cos = ca[:, None, :] * cb[None] - sa[:, None, :] * sb[None]
        sin = sa[:, None, :] * cb[None] + ca[:, None, :] * sb[None]
        return cos.reshape(n_f, n), sin.reshape(n_f, n)

    half = seq // 2
    tm = _tile(half, FOURIER_TILE)
    n_tiles = half // tm
    cl, sl = cs(half, seq)
    nyq = jnp.where(jnp.arange(seq) % 2 == 0, 1.0, -1.0).astype(f32)
    extra = jnp.zeros((n_tiles, FOURIER_EXTRA, seq), f32).at[:, 0, :].set(nyq)
    pos = jnp.concatenate([cl.reshape(n_tiles, tm, seq), sl.reshape(n_tiles, tm, seq), extra], axis=1)

    cc, sc = cs(GROUP_DIM, GROUP_DIM)
    norm = (seq * GROUP_DIM) ** -0.5
    mix = jnp.concatenate([jnp.concatenate([cc, cc], 1), jnp.concatenate([-sc, sc], 1)], 0) * norm

    s = jnp.arange(tm)
    flip = ((s[:, None] + s[None, :] == tm) & (s[:, None] >= 1)).astype(bf16)
    return pos.astype(bf16), mix.astype(bf16), flip


def _fourier_kernel(z_ref, pos_ref, mix_ref, flip_ref, o_ref, stash, *, seq, tm, n_tiles):
    j = pl.program_id(1)
    m = n_tiles - 1 - j

    def mixed(cz, sz):
        qs, ps = [], []
        for g in range(N_GROUPS):
            cols = slice(g * GROUP_DIM, (g + 1) * GROUP_DIM)
            szg = jnp.zeros_like(cz[:, cols]) if sz is None else sz[:, cols]
            lhs = jnp.concatenate([cz[:, cols], szg], axis=1).astype(bf16)
            out = jnp.dot(lhs, mix_ref[...], preferred_element_type=f32)
            qs.append(out[:, :GROUP_DIM])
            ps.append(out[:, GROUP_DIM:])
        return jnp.concatenate(qs, axis=1), jnp.concatenate(ps, axis=1)

    r = jnp.dot(pos_ref[...], z_ref[...], preferred_element_type=f32)
    q, p = mixed(r[:tm], r[tm:2 * tm])

    @pl.when(j == 0)
    def _():
        stash[...] = mixed(r[2 * tm:], None)[1]

    above = stash[0:1, :]
    o_ref[pl.ds(pl.multiple_of(m * tm, tm), tm), :] = q.astype(bf16)
    mir = jnp.dot(flip_ref[...], p.astype(bf16), preferred_element_type=f32)
    row = lax.broadcasted_iota(jnp.int32, mir.shape, 0)
    mir = jnp.where(row == 0, above, mir)
    o_ref[pl.ds(pl.multiple_of(seq // 2 + j * tm, tm), tm), :] = mir.astype(bf16)
    stash[...] = p[:FOURIER_EXTRA]


def _fourier(proj, rows, tables, latent):
    seq = rows.seq if latent else rows.ctx_len
    pos_tab, mix_tab, flip_tab = tables
    n_tiles, tab_rows, _ = pos_tab.shape
    tm = flip_tab.shape[0]
    row0 = 0 if latent else rows.m_x // seq
    in_specs = [pl.BlockSpec((seq, MIX_W), lambda b, j: (row0 + b, COL_F)),
                pl.BlockSpec((None, tab_rows, seq), lambda b, j: (n_tiles - 1 - j, 0, 0)),
                pl.BlockSpec((2 * GROUP_DIM, 2 * GROUP_DIM), lambda b, j: (0, 0)),
                pl.BlockSpec((tm, tm), lambda b, j: (0, 0))]
    return pl.pallas_call(
        functools.partial(_fourier_kernel, seq=seq, tm=tm, n_tiles=n_tiles),
        grid=(rows.batch, n_tiles),
        in_specs=in_specs,
        out_specs=pl.BlockSpec((seq, MIX_W), lambda b, j: (b, 0)),
        out_shape=jax.ShapeDtypeStruct((rows.batch * seq, MIX_W), bf16),
        scratch_shapes=[pltpu.VMEM((FOURIER_EXTRA, MIX_W), f32)],
        compiler_params=_cparams("parallel", "arbitrary"),
        name="fourier_x" if latent else "fourier_ctx",
    )(proj, pos_tab, mix_tab, flip_tab)


def _sgu_kernel(u_ref, v_ref, w_ref, b_ref, o_ref, *, tm):
    for ch in range(tm // CHUNK):
        rows = slice(ch * CHUNK, (ch + 1) * CHUNK)
        u = jax.nn.gelu(u_ref[rows, :].astype(f32))
        v = jax.nn.gelu(v_ref[rows, :].astype(f32))
        outs = []
        for g in range(N_GROUPS):
            cols = slice(g * GROUP_DIM, (g + 1) * GROUP_DIM)
            vg = v[:, cols]
            dv = vg - jnp.mean(vg, axis=-1, keepdims=True)
            vn = dv * lax.rsqrt(jnp.mean(dv * dv, axis=-1, keepdims=True) + EPS)
            s = jnp.dot(w_ref[g], vn.astype(bf16), preferred_element_type=f32) + b_ref[:, cols]
            outs.append(u[:, cols] * s)
        o_ref[rows, :] = jnp.concatenate(outs, axis=1).astype(bf16)


def _conv_kernel(b_ref, c_ref, x_ref, w_ref, o_ref, *, tm, n_x_tiles, ctx_len):
    y = c_ref[...].astype(f32) * x_ref[...].astype(f32)
    r = lax.broadcasted_iota(jnp.int32, y.shape, 0)
    period = jnp.where(pl.program_id(0) < n_x_tiles, GRID_W, ctx_len)
    rp = r & (period - 1)
    prev = jnp.where(rp == 0, 0.0, pltpu.roll(y, 1, axis=0))
    nxt = jnp.where(rp == period - 1, 0.0, pltpu.roll(y, tm - 1, axis=0))
    conv = w_ref[0:1, :] * prev + w_ref[1:2, :] * y + w_ref[2:3, :] * nxt
    o_ref[...] = (b_ref[...].astype(f32) * conv).astype(bf16)


def _local_mix_kernel(u_ref, v_ref, ws_ref, bs_ref, b_ref, c_ref, x_ref, wc_ref, oc_ref, od_ref,
                      *, tm, n_x_tiles, ctx_len):
    _sgu_kernel(u_ref, v_ref, ws_ref, bs_ref, oc_ref, tm=tm)
    _conv_kernel(b_ref, c_ref, x_ref, wc_ref, od_ref, tm=tm, n_x_tiles=n_x_tiles, ctx_len=ctx_len)


def _local_mix(proj, sgu_w, layer, sgu_bias, conv_w, rows, tm, n_rows):
    assert GRID_W & (GRID_W - 1) == 0 and rows.ctx_len & (rows.ctx_len - 1) == 0
    assert tm % GRID_W == 0 and tm % rows.ctx_len == 0 and tm % CHUNK == 0

    def col(c):
        return pl.BlockSpec((tm, MIX_W), lambda i: (i, c))

    out = jax.ShapeDtypeStruct((n_rows, MIX_W), bf16)
    return pl.pallas_call(
        functools.partial(_local_mix_kernel, tm=tm, n_x_tiles=rows.m_x // tm, ctx_len=rows.ctx_len),
        grid=(n_rows // tm,),
        in_specs=[col(COL_U), col(COL_V),
                  pl.BlockSpec((None, N_GROUPS, CHUNK, CHUNK), lambda i: (layer, 0, 0, 0)),
                  pl.BlockSpec((CHUNK, MIX_W), lambda i: (0, 0)),
                  col(COL_SB), col(COL_SC), col(COL_SX),
                  pl.BlockSpec((None, 3, MIX_W), lambda i: (layer, 0, 0))],
        out_specs=[col(0), col(0)],
        out_shape=[out, out],
        compiler_params=_cparams("parallel"),
        name="local_mix",
    )(proj, proj, sgu_w, sgu_bias, proj, proj, proj, conv_w)


def _merge_kernel(hx_ref, yax_ref, yac_ref, ybx_ref, ybc_ref, yc_ref, yd_ref, wg_ref, wb_ref, o_ref, acc,
                  *, row_chunk, n_x_tiles):
    n = pl.program_id(2)
    latent = pl.program_id(0) < n_x_tiles
    branches = ((yax_ref, yac_ref), (ybx_ref, ybc_ref), (yc_ref, None), (yd_ref, None))
    for k, (y_ref, y_ctx_ref) in enumerate(branches):
        @pl.when(n == k)
        def _(k=k, y_ref=y_ref, y_ctx_ref=y_ctx_ref):
            for r in range(hx_ref.shape[0] // row_chunk):
                rows = slice(r * row_chunk, (r + 1) * row_chunk)
                y = y_ref[rows, :]
                if y_ctx_ref is not None:
                    y = jnp.where(latent, y, y_ctx_ref[rows, :])
                gate = jax.nn.sigmoid(jnp.dot(hx_ref[rows, :], wg_ref[...], preferred_element_type=f32))
                term = gate * jnp.dot(y, wb_ref[...], preferred_element_type=f32)
                if k == 0:
                    acc[rows, :] = term
                elif k < N_BRANCH - 1:
                    acc[rows, :] += term
                else:
                    o_ref[rows, :] = (acc[rows, :] + term).astype(bf16)


def _merge(hx, ys, w_in, w_branch, layer, rows, tn, n_rows):
    d = hx.shape[1]
    m = n_rows
    tm = rows.tm
    nxt = rows.n_x_tiles
    nj = d // tn
    gate0 = N_MIX_COLS // tn
    y_spec = pl.BlockSpec((tm, MIX_W), lambda i, j, n: (i, 0))
    yx_spec = pl.BlockSpec((tm, MIX_W), lambda i, j, n: (jnp.minimum(i, nxt - 1), 0))
    yc_spec = pl.BlockSpec((tm, MIX_W), lambda i, j, n: (jnp.maximum(i - nxt, 0), 0))
    return pl.pallas_call(
        functools.partial(_merge_kernel, row_chunk=_tile(tm, MERGE_ROW_CHUNK), n_x_tiles=nxt),
        grid=(m // tm, nj, N_BRANCH),
        in_specs=[pl.BlockSpec((tm, d), lambda i, j, n: (i, 0)), yx_spec, yc_spec, yx_spec, yc_spec, y_spec, y_spec,
                  pl.BlockSpec((None, d, tn), lambda i, j, n: (layer, 0, gate0 + n * nj + j)),
                  pl.BlockSpec((None, None, MIX_W, tn), lambda i, j, n: (layer, n, 0, j))],
        out_specs=pl.BlockSpec((tm, tn), lambda i, j, n: (i, j)),
        out_shape=jax.ShapeDtypeStruct((m, d), bf16),
        scratch_shapes=[pltpu.VMEM((tm, tn), f32)],
        compiler_params=_cparams("parallel", "arbitrary", "arbitrary"),
        name="merge",
    )(hx, *ys, w_in, w_branch)


def _outproj_kernel(m_ref, w_ref, x_ref, g1_ref, g2_ref, gate_ref, sh_ref, sc_ref, xo_ref, ho_ref, *, row_chunk):
    res_scale = gate_ref[...] * g1_ref[...]
    mod_scale = g2_ref[...] * (1.0 + sc_ref[...])
    for r in range(m_ref.shape[0] // row_chunk):
        rows = slice(r * row_chunk, (r + 1) * row_chunk)
        mix = jnp.dot(m_ref[rows, :], w_ref[...], preferred_element_type=f32)
        xn = x_ref[rows, :] + _rms(mix) * res_scale
        xo_ref[rows, :] = xn
        ho_ref[rows, :] = (_rms(xn) * mod_scale + sh_ref[...]).astype(bf16)


def _outproj(merged, w_out, xs, g1, g2, mods, layer, rows, n_rows):
    d = xs.shape[1]
    tm = rows.tm
    row = pl.BlockSpec((tm, d), lambda i: (i, 0))
    return pl.pallas_call(
        functools.partial(_outproj_kernel, row_chunk=_tile(tm, OUTPROJ_ROW_CHUNK)),
        grid=(n_rows // tm,),
        in_specs=[row, pl.BlockSpec((None, d, d), lambda i: (layer, 0, 0)), row, _vec_spec(d), _vec_spec(d),
                  rows.mod_spec(layer, 2, d), rows.mod_spec(layer, 3, d), rows.mod_spec(layer, 4, d)],
        out_specs=[row, row],
        out_shape=[jax.ShapeDtypeStruct((n_rows, d), f32), jax.ShapeDtypeStruct((n_rows, d), bf16)],
        compiler_params=_cparams("parallel"),
        name="outproj",
    )(merged, w_out, xs, g1, g2, mods, mods, mods)


def _ffn_kernel(*refs, n_f, n_tiles, has_next, row_chunk):
    if has_next:
        (h_ref, wa_ref, wb_ref, w2_ref, x_hbm, g3_ref, gate_ref, gn_ref, sh_ref, sc_ref,
         xo_hbm, ho_hbm, acc, xbuf, hbuf, sems) = refs
    else:
        h_ref, wa_ref, wb_ref, w2_ref, x_hbm, g3_ref, gate_ref, xo_hbm, acc, xbuf, sems = refs
    i, f = pl.program_id(0), pl.program_id(1)
    tm = h_ref.shape[0]

    def tile_rows(t):
        return pl.ds(pl.multiple_of(t * tm, tm), tm)

    def x_in(t):
        return pltpu.make_async_copy(x_hbm.at[tile_rows(t), :], xbuf, sems.at[0])

    def outs(t):
        copies = [pltpu.make_async_copy(xbuf, xo_hbm.at[tile_rows(t), :], sems.at[1])]
        if has_next:
            copies.append(pltpu.make_async_copy(hbuf, ho_hbm.at[tile_rows(t), :], sems.at[2]))
        return copies

    @pl.when(f == 1)
    def _():
        @pl.when(i > 0)
        def _():
            for cp in outs(i - 1):
                cp.wait()
        x_in(i).start()

    def step(first, last):
        if last:
            x_in(i).wait()
            res_scale = gate_ref[...] * g3_ref[...]
            if has_next:
                mod_scale = gn_ref[...] * (1.0 + sc_ref[...])
        for r in range(tm // row_chunk):
            rows = slice(r * row_chunk, (r + 1) * row_chunk)
            h = h_ref[rows, :]
            a = jnp.dot(h, wa_ref[...], preferred_element_type=f32)
            b = jnp.dot(h, wb_ref[...], preferred_element_type=f32)
            p = jnp.dot((jax.nn.silu(a) * b).astype(bf16), w2_ref[...], preferred_element_type=f32)
            if not first:
                p = acc[rows, :] + p
            if not last:
                acc[rows, :] = p
            else:
                xn = xbuf[rows, :] + _rms(p) * res_scale
                xbuf[rows, :] = xn
                if has_next:
                    hbuf[rows, :] = (_rms(xn) * mod_scale + sh_ref[...]).astype(bf16)
        if last:
            for cp in outs(i):
                cp.start()

            @pl.when(i == n_tiles - 1)
            def _():
                for cp in outs(i):
                    cp.wait()

    pl.when(f == 0)(lambda: step(True, False))
    pl.when(f == n_f - 1)(lambda: step(False, True))
    pl.when((f > 0) & (f < n_f - 1))(lambda: step(False, False))


def _ffn(hx, w1, w2, xs, g3, g_next, mods, layer, rows, tf, n_rows):
    d = xs.shape[1]
    n_f = w2.shape[1] // tf
    assert n_f >= 3
    tm = rows.tm
    n_tiles = n_rows // tm
    has_next = g_next is not None
    row = pl.BlockSpec((tm, d), lambda i, f: (i, 0))
    hbm = pl.BlockSpec(memory_space=pl.ANY)
    in_specs = [row,
                pl.BlockSpec((None, d, tf), lambda i, f: (layer, 0, f)),
                pl.BlockSpec((None, d, tf), lambda i, f: (layer, 0, n_f + f)),
                pl.BlockSpec((None, tf, d), lambda i, f: (layer, f, 0)),
                hbm, _vec_spec(d), rows.mod_spec(layer, 5, d)]
    args = [hx, w1, w1, w2, xs, g3, mods]
    out_specs = [hbm]
    out_shape = [jax.ShapeDtypeStruct((n_rows, d), f32)]
    scratch = [pltpu.VMEM((tm, d), f32), pltpu.VMEM((tm, d), f32)]
    if has_next:
        in_specs += [_vec_spec(d), rows.mod_spec(layer + 1, 0, d), rows.mod_spec(layer + 1, 1, d)]
        args += [g_next, mods, mods]
        out_specs.append(hbm)
        out_shape.append(jax.ShapeDtypeStruct((n_rows, d), bf16))
        scratch.append(pltpu.VMEM((tm, d), bf16))
    scratch.append(pltpu.SemaphoreType.DMA((3,)))
    out = pl.pallas_call(
        functools.partial(_ffn_kernel, n_f=n_f, n_tiles=n_tiles, has_next=has_next,
                          row_chunk=_tile(tm, FFN_ROW_CHUNK)),
        grid=(n_tiles, n_f),
        in_specs=in_specs,
        out_specs=out_specs,
        out_shape=out_shape,
        scratch_shapes=scratch,
        compiler_params=_cparams("arbitrary", "arbitrary"),
        name="ffn",
    )(*args)
    return out if has_next else (out[0], None)


def kernel(x, c, ctx, c_ctx, ada_w, ada_b, norm_g, w_in, ret_log_decay, conv_w, sgu_w, sgu_b,
           w_branch, w_out, ffn_w_in, ffn_w_out):
    batch, seq, d = x.shape
    ctx_len = ctx.shape[1]
    depth = ada_w.shape[0]
    d_ff = ffn_w_out.shape[1]
    assert batch + 1 <= MOD_ROWS and w_in.shape[2] == N_MIX_COLS + N_BRANCH * d

    rows = _Rows(batch, seq, ctx_len, _tile(batch * ctx_len, 512))
    tm_big = _tile(batch * ctx_len, 1024)
    rows_big = _Rows(batch, seq, ctx_len, tm_big)

    cc = jnp.zeros((MOD_ROWS, d), f32).at[:batch].set(c).at[batch].set(c_ctx)
    mods = _adaln(cc, ada_w, ada_b).reshape(depth * MOD_ROWS * 6, 1, d)

    rope_tabs = _rope_tables(seq)
    dft_x = _dft_tables(seq)
    dft_c = _dft_tables(ctx_len)

    xs, hx = _norm_mod(x.reshape(batch * seq, d), ctx.reshape(batch * ctx_len, d),
                       norm_g[0, 0:1], mods, rows)

    w_in_b, w_branch_b, w_out_b = w_in.astype(bf16), w_branch.astype(bf16), w_out.astype(bf16)
    ffn_w_in_b, ffn_w_out_b, sgu_w_b = ffn_w_in.astype(bf16), ffn_w_out.astype(bf16), sgu_w.astype(bf16)
    tn = _tile(d, 1024)
    assert N_MIX_COLS % tn == 0

    for l in range(depth):
        sgu_bias = jnp.repeat(jnp.transpose(sgu_b[l]), GROUP_DIM, axis=1)

        last = l + 1 == depth
        n_rows = rows.m_x if last else rows.m

        proj = _matmul(hx, w_in_b, l, N_MIX_COLS, tm_big, N_MIX_COLS // 2, "inproj")
        ya = _retention(proj, ret_log_decay[l], rows, rope_tabs, True)
        yb = _fourier(proj, rows, dft_x, True)
        if last:
            ya_c, yb_c = ya, yb
        else:
            ya_c = _retention(proj, ret_log_decay[l], rows, None, False)
            yb_c = _fourier(proj, rows, dft_c, False)
        yc, yd = _local_mix(proj, sgu_w_b, l, sgu_bias, conv_w, rows, rows.tm, n_rows)
        merged = _merge(hx, (ya, ya_c, yb, yb_c, yc, yd), w_in_b, w_branch_b, l, rows_big, tn, n_rows)
        xs, hx2 = _outproj(merged, w_out_b, xs, norm_g[l, 1:2], norm_g[l, 2:3], mods, l, rows, n_rows)
        g_next = None if last else norm_g[l + 1, 0:1]
        xs, hx = _ffn(hx2, ffn_w_in_b, ffn_w_out_b, xs, norm_g[l, 3:4], g_next, mods, l, rows_big,
                      _tile(d_ff, 512), n_rows)

    return xs.reshape(batch, seq, d)
```

```python
import functools

import jax
import jax.numpy as jnp
from jax import lax
from jax.experimental import pallas as pl
from jax.experimental.pallas import tpu as pltpu

f32 = jnp.float32
bf16 = jnp.bfloat16

GRID_W = 64
MIX_W = 512
N_BRANCH = 4
GROUP_DIM = 128
N_GROUPS = MIX_W // GROUP_DIM
CHUNK = 128
ROPE_BASE = 10000.0
EPS = 1e-6
N_MIX_COLS = 10 * MIX_W
COL_F, COL_U, COL_V, COL_SB, COL_SC, COL_SX = 4, 5, 6, 7, 8, 9
MOD_ROWS = 16
CHUNK_UNROLL = 16
FFN_ROW_CHUNK = 256
MERGE_ROW_CHUNK = 512
OUTPROJ_ROW_CHUNK = 256

VMEM_LIMIT_BYTES = 56 * 1024 * 1024


def _cparams(*sem):
    return pltpu.CompilerParams(dimension_semantics=sem, vmem_limit_bytes=VMEM_LIMIT_BYTES)


def _tile(n, pref):
    t = min(n, pref)
    while n % t:
        t //= 2
    return t


def _rms(x):
    return x * lax.rsqrt(jnp.mean(x * x, axis=-1, keepdims=True) + EPS)


def _adaln_kernel(c_ref, w_ref, b_ref, o_ref):
    s = jax.nn.silu(c_ref[...]).astype(bf16)
    o_ref[...] = jnp.dot(s, w_ref[...].astype(bf16), preferred_element_type=f32) + b_ref[...]


def _adaln(cc, ada_w, ada_b):
    depth, d, n6 = ada_w.shape
    tn = _tile(n6, 1024)
    return pl.pallas_call(
        _adaln_kernel,
        grid=(depth, n6 // tn),
        in_specs=[
            pl.BlockSpec((MOD_ROWS, d), lambda l, j: (0, 0)),
            pl.BlockSpec((None, d, tn), lambda l, j: (l, 0, j)),
            pl.BlockSpec((None, 1, tn), lambda l, j: (l, 0, j)),
        ],
        out_specs=pl.BlockSpec((None, MOD_ROWS, tn), lambda l, j: (l, 0, j)),
        out_shape=jax.ShapeDtypeStruct((depth, MOD_ROWS, n6), f32),
        compiler_params=_cparams("parallel", "parallel"),
        name="adaln",
    )(cc, ada_w, ada_b.reshape(depth, 1, n6))


class _Rows:
    def __init__(self, batch, seq, ctx_len, tm):
        self.batch, self.seq, self.ctx_len, self.tm = batch, seq, ctx_len, tm
        self.m_x, self.m_c = batch * seq, batch * ctx_len
        self.m = self.m_x + self.m_c
        assert seq % tm == 0 and self.m_c % tm == 0
        self.n_x_tiles = self.m_x // tm
        self.n_tiles = self.m // tm
        self.per_batch = seq // tm

    def mod_spec(self, layer, k, d, extra_axes=0):
        def index(i, *_):
            r = jnp.where(i < self.n_x_tiles, i // self.per_batch, self.batch)
            return ((layer * MOD_ROWS + r) * 6 + k, 0, 0)
        return pl.BlockSpec((None, 1, d), index)


def _vec_spec(d):
    return pl.BlockSpec((1, d), lambda i, *_: (0, 0))


def _norm_mod_kernel(x_ref, c_ref, g_ref, sh_ref, sc_ref, hx_ref, *, n_x_tiles):
    def emit(x):
        hx_ref[...] = (_rms(x) * (g_ref[...] * (1.0 + sc_ref[...])) + sh_ref[...]).astype(bf16)

    @pl.when(pl.program_id(0) < n_x_tiles)
    def _():
        emit(x_ref[...])

    @pl.when(pl.program_id(0) >= n_x_tiles)
    def _():
        emit(c_ref[...])


def _norm_mod(x2d, c2d, g, mods, rows):
    d = x2d.shape[1]
    tm, nxt = rows.tm, rows.n_x_tiles
    return pl.pallas_call(
        functools.partial(_norm_mod_kernel, n_x_tiles=nxt),
        grid=(rows.n_tiles,),
        in_specs=[
            pl.BlockSpec((tm, d), lambda i: (jnp.minimum(i, nxt - 1), 0)),
            pl.BlockSpec((tm, d), lambda i: (jnp.maximum(i - nxt, 0), 0)),
            _vec_spec(d),
            rows.mod_spec(0, 0, d),
            rows.mod_spec(0, 1, d),
        ],
        out_specs=pl.BlockSpec((tm, d), lambda i: (i, 0)),
        out_shape=jax.ShapeDtypeStruct((rows.m, d), bf16),
        compiler_params=_cparams("parallel"),
        name="norm_mod",
    )(x2d, c2d, g, mods, mods)


def _matmul_kernel(a_ref, b_ref, o_ref):
    o_ref[...] = jnp.dot(a_ref[...], b_ref[...], preferred_element_type=f32).astype(o_ref.dtype)


def _matmul(a, b_stack, layer, n, tm, tn, name):
    m, k = a.shape
    return pl.pallas_call(
        _matmul_kernel,
        grid=(m // tm, n // tn),
        in_specs=[pl.BlockSpec((tm, k), lambda i, j: (i, 0)),
                  pl.BlockSpec((None, k, tn), lambda i, j: (layer, 0, j))],
        out_specs=pl.BlockSpec((tm, tn), lambda i, j: (i, j)),
        out_shape=jax.ShapeDtypeStruct((m, n), bf16),
        compiler_params=_cparams("parallel", "arbitrary"),
        name=name,
    )(a, b_stack)


def _rope_tables(seq):
    t = jnp.arange(seq)
    n = GROUP_DIM // 2
    freqs = ROPE_BASE ** (-jnp.arange(0, n, 2, dtype=f32) / n)

    def tab(pos):
        ang = pos.astype(f32)[:, None] * freqs[None, :]
        c, s = jnp.cos(ang), jnp.sin(ang)
        return jnp.concatenate([c, c], -1), jnp.concatenate([-s, s], -1)

    c1, s1 = tab(t // GRID_W)
    c2, s2 = tab(t % GRID_W)
    dst = jnp.arange(GROUP_DIM)
    src = jnp.where((dst % n) < n // 2, dst + n // 2, dst - n // 2)
    perm = (jnp.arange(GROUP_DIM)[:, None] == src[None, :]).astype(bf16)
    return jnp.concatenate([c1, c2], -1), jnp.concatenate([s1, s2], -1), perm


def _retention_kernel(*refs, n_chunks, rope, ctx_len):
    refs = list(refs)
    lg_ref, q_ref, k_ref, v_ref, g_ref = refs[:5]
    pos = 5
    if ctx_len:
        kc_ref, vc_ref = refs[pos:pos + 2]
        pos += 2
    if rope:
        cos_ref, sin_ref, perm_ref = refs[pos:pos + 3]
        pos += 3
    o_ref, qs, ks, sf, sb = refs[pos:]

    c_len, dh = CHUNK, GROUP_DIM
    h = pl.program_id(1)
    lgf, lgb = lg_ref[0, h], lg_ref[1, h]
    scale = dh ** -0.5
    ri = lax.broadcasted_iota(jnp.int32, (c_len, dh), 0).astype(f32)
    ci = lax.broadcasted_iota(jnp.int32, (c_len, dh), 1).astype(f32)
    rel = ri - ci
    decay = (jnp.where(rel >= 0, jnp.exp(lgf * jnp.maximum(rel, 0.0)), 0.0)
             + jnp.where(rel <= 0, jnp.exp(lgb * jnp.maximum(-rel, 0.0)), 0.0))
    kw_f = jnp.exp(lgf * (c_len - 1 - ri))
    kw_b = jnp.exp(lgb * ri)
    qw_f = jnp.exp(lgf * (ri + 1.0))
    qw_b = jnp.exp(lgb * (c_len - ri))
    cd_f = jnp.exp(jnp.full((dh, dh), lgf * c_len, f32))
    cd_b = jnp.exp(jnp.full((dh, dh), lgb * c_len, f32))

    def rows_of(c):
        return pl.ds(pl.multiple_of(c * c_len, c_len), c_len)

    def kt_dot(kb, vw):
        return lax.dot_general(kb, vw, (((0,), (0,)), ((), ())), preferred_element_type=f32)

    def prep(c, carry):
        rows = rows_of(c)
        q, k = q_ref[rows, :], k_ref[rows, :]
        qf, kf = q.astype(f32), k.astype(f32)
        if rope:
            cos, sin = cos_ref[rows, :], sin_ref[rows, :]
            qf = qf * cos + jnp.dot(q, perm_ref[...], preferred_element_type=f32) * sin
            kf = kf * cos + jnp.dot(k, perm_ref[...], preferred_element_type=f32) * sin
        kb = (kf * scale).astype(bf16)
        qs[rows, :] = qf.astype(bf16)
        ks[rows, :] = kb
        v = v_ref[rows, :].astype(f32)
        vw = jnp.concatenate([v * kw_f, v * kw_b], axis=1).astype(bf16)
        ds = kt_dot(kb, vw)
        sf[c] = ds[:, :dh]
        sb[c] = ds[:, dh:]
        return carry

    unroll = min(n_chunks, CHUNK_UNROLL)
    lax.fori_loop(0, n_chunks, prep, 0, unroll=unroll)

    if ctx_len:
        mi = lax.broadcasted_iota(jnp.int32, (ctx_len, dh), 0).astype(f32)
        kcb = (kc_ref[...].astype(f32) * scale).astype(bf16)
        vc = vc_ref[...].astype(f32)
        vcw = jnp.concatenate([vc * jnp.exp(lgf * (ctx_len - 1 - mi)), vc * jnp.exp(lgb * mi)],
                              axis=1).astype(bf16)
        s0 = kt_dot(kcb, vcw)
        s0_f, s0_b = s0[:, :dh], s0[:, dh:]
    else:
        s0_f = s0_b = jnp.zeros((dh, dh), f32)

    def scan_f(c, s):
        nxt = cd_f * s + sf[c]
        sf[c] = s
        return nxt

    def scan_b(j, s):
        c = n_chunks - 1 - j
        nxt = cd_b * s + sb[c]
        sb[c] = s
        return nxt

    lax.fori_loop(0, n_chunks, scan_f, s0_f)
    lax.fori_loop(0, n_chunks, scan_b, s0_b)

    def emit(c, carry):
        rows = rows_of(c)
        q, k, v = qs[rows, :], ks[rows, :], v_ref[rows, :]
        sc = lax.dot_general(q, k, (((1,), (1,)), ((), ())), preferred_element_type=f32) * decay
        inner = jnp.dot(sc.astype(bf16), v, preferred_element_type=f32)
        s2 = jnp.concatenate([sf[c], sb[c]], axis=1).astype(bf16)
        cr = jnp.dot(q, s2, preferred_element_type=f32)
        ret = inner + qw_f * cr[:, :dh] + qw_b * cr[:, dh:]
        g = g_ref[rows, :].astype(f32)
        o_ref[rows, :] = (jax.nn.silu(g) * _rms(ret)).astype(bf16)
        return carry

    lax.fori_loop(0, n_chunks, emit, 0, unroll=unroll)


def _retention(proj, log_decay, rows, rope_tabs, latent):
    seq = rows.seq if latent else rows.ctx_len
    n_chunks = seq // CHUNK
    row0 = 0 if latent else rows.m_x // seq
    hq = MIX_W // GROUP_DIM

    def col(off):
        return pl.BlockSpec((seq, GROUP_DIM), lambda b, h: (row0 + b, off * hq + h))

    in_specs = [pl.BlockSpec(memory_space=pltpu.SMEM), col(0), col(1), col(2), col(3)]
    args = [log_decay, proj, proj, proj, proj]
    if latent:
        cb0 = rows.m_x // rows.ctx_len
        for off in (1, 2):
            in_specs.append(pl.BlockSpec((rows.ctx_len, GROUP_DIM),
                                         lambda b, h, off=off: (cb0 + b, off * hq + h)))
            args.append(proj)
        cos, sin, perm = rope_tabs
        in_specs += [pl.BlockSpec((seq, GROUP_DIM), lambda b, h: (0, 0)),
                     pl.BlockSpec((seq, GROUP_DIM), lambda b, h: (0, 0)),
                     pl.BlockSpec((GROUP_DIM, GROUP_DIM), lambda b, h: (0, 0))]
        args += [cos, sin, perm]

    return pl.pallas_call(
        functools.partial(_retention_kernel, n_chunks=n_chunks, rope=latent,
                          ctx_len=rows.ctx_len if latent else 0),
        grid=(rows.batch, hq),
        in_specs=in_specs,
        out_specs=pl.BlockSpec((seq, GROUP_DIM), lambda b, h: (b, h)),
        out_shape=jax.ShapeDtypeStruct((rows.batch * seq, MIX_W), bf16),
        scratch_shapes=[pltpu.VMEM((seq, GROUP_DIM), bf16), pltpu.VMEM((seq, GROUP_DIM), bf16),
                        pltpu.VMEM((n_chunks, GROUP_DIM, GROUP_DIM), f32),
                        pltpu.VMEM((n_chunks, GROUP_DIM, GROUP_DIM), f32)],
        compiler_params=_cparams("parallel", "parallel"),
        name="retention_x" if latent else "retention_ctx",
    )(*args)


FOURIER_TILE = 512
FOURIER_EXTRA = 16


def _dft_tables(seq):
    def phase(k, n):
        ang = (k % n).astype(f32) * (2.0 * jnp.pi / n)
        return jnp.cos(ang), jnp.sin(ang)

    def cs(n_f, n):
        r = 1 << ((n_f.bit_length() - 1) // 2)
        t = jnp.arange(n, dtype=jnp.int32)
        ca, sa = phase((r * jnp.arange(n_f // r, dtype=jnp.int32))[:, None] * t[None, :], n)
        cb, sb = phase(jnp.arange(r, dtype=jnp.int32)[:, None] * t[None, :], n)
        cos = ca[:, None, :] * cb[None] - sa[:, None, :] * sb[None]
        sin = sa[:, None, :] * cb[None] + ca[:, None, :] * sb[None]
        return cos.reshape(n_f, n), sin.reshape(n_f, n)

    half = seq // 2
    tm = _tile(half, FOURIER_TILE)
    n_tiles = half // tm
    cl, sl = cs(half, seq)
    nyq = jnp.where(jnp.arange(seq) % 2 == 0, 1.0, -1.0).astype(f32)
    extra = jnp.zeros((n_tiles, FOURIER_EXTRA, seq), f32).at[:, 0, :].set(nyq)
    pos = jnp.concatenate([cl.reshape(n_tiles, tm, seq), sl.reshape(n_tiles, tm, seq), extra], axis=1)

    cc, sc = cs(GROUP_DIM, GROUP_DIM)
    norm = (seq * GROUP_DIM) ** -0.5
    mix = jnp.concatenate([jnp.concatenate([cc, cc], 1), jnp.concatenate([-sc, sc], 1)], 0) * norm

    s = jnp.arange(tm)
    flip = ((s[:, None] + s[None, :] == tm) & (s[:, None] >= 1)).astype(bf16)
    return pos.astype(bf16), mix.astype(bf16), flip


def _fourier_kernel(z_ref, pos_ref, mix_ref, flip_ref, o_ref, stash, *, seq, tm, n_tiles):
    j = pl.program_id(1)
    m = n_tiles - 1 - j

    def mixed(cz, sz):
        qs, ps = [], []
        for g in range(N_GROUPS):
            cols = slice(g * GROUP_DIM, (g + 1) * GROUP_DIM)
            szg = jnp.zeros_like(cz[:, cols]) if sz is None else sz[:, cols]
            lhs = jnp.concatenate([cz[:, cols], szg], axis=1).astype(bf16)
            out = jnp.dot(lhs, mix_ref[...], preferred_element_type=f32)
            qs.append(out[:, :GROUP_DIM])
            ps.append(out[:, GROUP_DIM:])
        return jnp.concatenate(qs, axis=1), jnp.concatenate(ps, axis=1)

    r = jnp.dot(pos_ref[...], z_ref[...], preferred_element_type=f32)
    q, p = mixed(r[:tm], r[tm:2 * tm])

    @pl.when(j == 0)
    def _():
        stash[...] = mixed(r[2 * tm:], None)[1]

    above = stash[0:1, :]
    o_ref[pl.ds(pl.multiple_of(m * tm, tm), tm), :] = q.astype(bf16)
    mir = jnp.dot(flip_ref[...], p.astype(bf16), preferred_element_type=f32)
    row = lax.broadcasted_iota(jnp.int32, mir.shape, 0)
    mir = jnp.where(row == 0, above, mir)
    o_ref[pl.ds(pl.multiple_of(seq // 2 + j * tm, tm), tm), :] = mir.astype(bf16)
    stash[...] = p[:FOURIER_EXTRA]


def _fourier(proj, rows, tables, latent):
    seq = rows.seq if latent else rows.ctx_len
    pos_tab, mix_tab, flip_tab = tables
    n_tiles, tab_rows, _ = pos_tab.shape
    tm = flip_tab.shape[0]
    row0 = 0 if latent else rows.m_x // seq
    in_specs = [pl.BlockSpec((seq, MIX_W), lambda b, j: (row0 + b, COL_F)),
                pl.BlockSpec((None, tab_rows, seq), lambda b, j: (n_tiles - 1 - j, 0, 0)),
                pl.BlockSpec((2 * GROUP_DIM, 2 * GROUP_DIM), lambda b, j: (0, 0)),
                pl.BlockSpec((tm, tm), lambda b, j: (0, 0))]
    return pl.pallas_call(
        functools.partial(_fourier_kernel, seq=seq, tm=tm, n_tiles=n_tiles),
        grid=(rows.batch, n_tiles),
        in_specs=in_specs,
        out_specs=pl.BlockSpec((seq, MIX_W), lambda b, j: (b, 0)),
        out_shape=jax.ShapeDtypeStruct((rows.batch * seq, MIX_W), bf16),
        scratch_shapes=[pltpu.VMEM((FOURIER_EXTRA, MIX_W), f32)],
        compiler_params=_cparams("parallel", "arbitrary"),
        name="fourier_x" if latent else "fourier_ctx",
    )(proj, pos_tab, mix_tab, flip_tab)


def _sgu_kernel(u_ref, v_ref, w_ref, b_ref, o_ref, *, tm):
    for ch in range(tm // CHUNK):
        rows = slice(ch * CHUNK, (ch + 1) * CHUNK)
        u = jax.nn.gelu(u_ref[rows, :].astype(f32))
        v = jax.nn.gelu(v_ref[rows, :].astype(f32))
        outs = []
        for g in range(N_GROUPS):
            cols = slice(g * GROUP_DIM, (g + 1) * GROUP_DIM)
            vg = v[:, cols]
            dv = vg - jnp.mean(vg, axis=-1, keepdims=True)
            vn = dv * lax.rsqrt(jnp.mean(dv * dv, axis=-1, keepdims=True) + EPS)
            s = jnp.dot(w_ref[g], vn.astype(bf16), preferred_element_type=f32) + b_ref[:, cols]
            outs.append(u[:, cols] * s)
        o_ref[rows, :] = jnp.concatenate(outs, axis=1).astype(bf16)


def _conv_kernel(b_ref, c_ref, x_ref, w_ref, o_ref, *, tm, n_x_tiles, ctx_len):
    y = c_ref[...].astype(f32) * x_ref[...].astype(f32)
    r = lax.broadcasted_iota(jnp.int32, y.shape, 0)
    period = jnp.where(pl.program_id(0) < n_x_tiles, GRID_W, ctx_len)
    rp = r & (period - 1)
    prev = jnp.where(rp == 0, 0.0, pltpu.roll(y, 1, axis=0))
    nxt = jnp.where(rp == period - 1, 0.0, pltpu.roll(y, tm - 1, axis=0))
    conv = w_ref[0:1, :] * prev + w_ref[1:2, :] * y + w_ref[2:3, :] * nxt
    o_ref[...] = (b_ref[...].astype(f32) * conv).astype(bf16)


def _local_mix_kernel(u_ref, v_ref, ws_ref, bs_ref, b_ref, c_ref, x_ref, wc_ref, oc_ref, od_ref,
                      *, tm, n_x_tiles, ctx_len):
    _sgu_kernel(u_ref, v_ref, ws_ref, bs_ref, oc_ref, tm=tm)
    _conv_kernel(b_ref, c_ref, x_ref, wc_ref, od_ref, tm=tm, n_x_tiles=n_x_tiles, ctx_len=ctx_len)


def _local_mix(proj, sgu_w, layer, sgu_bias, conv_w, rows, tm, n_rows):
    assert GRID_W & (GRID_W - 1) == 0 and rows.ctx_len & (rows.ctx_len - 1) == 0
    assert tm % GRID_W == 0 and tm % rows.ctx_len == 0 and tm % CHUNK == 0

    def col(c):
        return pl.BlockSpec((tm, MIX_W), lambda i: (i, c))

    out = jax.ShapeDtypeStruct((n_rows, MIX_W), bf16)
    return pl.pallas_call(
        functools.partial(_local_mix_kernel, tm=tm, n_x_tiles=rows.m_x // tm, ctx_len=rows.ctx_len),
        grid=(n_rows // tm,),
        in_specs=[col(COL_U), col(COL_V),
                  pl.BlockSpec((None, N_GROUPS, CHUNK, CHUNK), lambda i: (layer, 0, 0, 0)),
                  pl.BlockSpec((CHUNK, MIX_W), lambda i: (0, 0)),
                  col(COL_SB), col(COL_SC), col(COL_SX),
                  pl.BlockSpec((None, 3, MIX_W), lambda i: (layer, 0, 0))],
        out_specs=[col(0), col(0)],
        out_shape=[out, out],
        compiler_params=_cparams("parallel"),
        name="local_mix",
    )(proj, proj, sgu_w, sgu_bias, proj, proj, proj, conv_w)


def _merge_kernel(hx_ref, yax_ref, yac_ref, ybx_ref, ybc_ref, yc_ref, yd_ref, wg0_ref, wg1_ref, wb0_ref, wb1_ref,
                  o_ref, acc, *, row_chunk, n_x_tiles):
    pair = pl.program_id(2)
    latent = pl.program_id(0) < n_x_tiles

    def term(rows, y, wg_ref, wb_ref):
        gate = jax.nn.sigmoid(jnp.dot(hx_ref[rows, :], wg_ref[...], preferred_element_type=f32))
        return gate * jnp.dot(y, wb_ref[...], preferred_element_type=f32)

    chunks = [slice(r * row_chunk, (r + 1) * row_chunk) for r in range(hx_ref.shape[0] // row_chunk)]

    @pl.when(pair == 0)
    def _():
        for rows in chunks:
            ya = jnp.where(latent, yax_ref[rows, :], yac_ref[rows, :])
            yb = jnp.where(latent, ybx_ref[rows, :], ybc_ref[rows, :])
            acc[rows, :] = term(rows, ya, wg0_ref, wb0_ref) + term(rows, yb, wg1_ref, wb1_ref)

    @pl.when(pair == 1)
    def _():
        for rows in chunks:
            t = acc[rows, :] + term(rows, yc_ref[rows, :], wg0_ref, wb0_ref)
            o_ref[rows, :] = (t + term(rows, yd_ref[rows, :], wg1_ref, wb1_ref)).astype(bf16)


def _merge(hx, ys, w_in, w_branch, layer, rows, tn, n_rows):
    assert N_BRANCH == 4
    d = hx.shape[1]
    m = n_rows
    tm = rows.tm
    nxt = rows.n_x_tiles
    nj = d // tn
    gate0 = N_MIX_COLS // tn
    y_spec = pl.BlockSpec((tm, MIX_W), lambda i, j, p: (i, 0))
    yx_spec = pl.BlockSpec((tm, MIX_W), lambda i, j, p: (jnp.minimum(i, nxt - 1), 0))
    yc_spec = pl.BlockSpec((tm, MIX_W), lambda i, j, p: (jnp.maximum(i - nxt, 0), 0))

    def wg_spec(k):
        return pl.BlockSpec((None, d, tn), lambda i, j, p: (layer, 0, gate0 + (2 * p + k) * nj + j))

    def wb_spec(k):
        return pl.BlockSpec((None, None, MIX_W, tn), lambda i, j, p: (layer, 2 * p + k, 0, j))

    return pl.pallas_call(
        functools.partial(_merge_kernel, row_chunk=_tile(tm, MERGE_ROW_CHUNK), n_x_tiles=nxt),
        grid=(m // tm, nj, N_BRANCH // 2),
        in_specs=[pl.BlockSpec((tm, d), lambda i, j, p: (i, 0)), yx_spec, yc_spec, yx_spec, yc_spec, y_spec, y_spec,
                  wg_spec(0), wg_spec(1), wb_spec(0), wb_spec(1)],
        out_specs=pl.BlockSpec((tm, tn), lambda i, j, p: (i, j)),
        out_shape=jax.ShapeDtypeStruct((m, d), bf16),
        scratch_shapes=[pltpu.VMEM((tm, tn), f32)],
        compiler_params=_cparams("parallel", "arbitrary", "arbitrary"),
        name="merge",
    )(hx, *ys, w_in, w_in, w_branch, w_branch)


def _outproj_kernel(m_ref, w_ref, xl_ref, xc_ref, g1_ref, g2_ref, gate_ref, sh_ref, sc_ref, xo_ref, ho_ref,
                    *, row_chunk, n_x_tiles):
    latent = pl.program_id(0) < n_x_tiles
    res_scale = gate_ref[...] * g1_ref[...]
    mod_scale = g2_ref[...] * (1.0 + sc_ref[...])
    for r in range(m_ref.shape[0] // row_chunk):
        rows = slice(r * row_chunk, (r + 1) * row_chunk)
        mix = jnp.dot(m_ref[rows, :], w_ref[...], preferred_element_type=f32)
        x = xl_ref[rows, :] if xc_ref is None else jnp.where(latent, xl_ref[rows, :], xc_ref[rows, :])
        xn = x + _rms(mix) * res_scale
        xo_ref[rows, :] = xn
        ho_ref[rows, :] = (_rms(xn) * mod_scale + sh_ref[...]).astype(bf16)


def _outproj(merged, w_out, xs, g1, g2, mods, layer, rows, n_rows):
    d = merged.shape[1]
    tm, nxt = rows.tm, rows.n_x_tiles
    row = pl.BlockSpec((tm, d), lambda i: (i, 0))
    if isinstance(xs, tuple):
        x_specs = [pl.BlockSpec((tm, d), lambda i: (jnp.minimum(i, nxt - 1), 0)),
                   pl.BlockSpec((tm, d), lambda i: (jnp.maximum(i - nxt, 0), 0))]
        kernel = functools.partial(_outproj_kernel, row_chunk=_tile(tm, OUTPROJ_ROW_CHUNK), n_x_tiles=nxt)
    else:
        xs, x_specs = (xs,), [row]

        def kernel(m_ref, w_ref, x_ref, *rest):
            _outproj_kernel(m_ref, w_ref, x_ref, None, *rest, row_chunk=_tile(tm, OUTPROJ_ROW_CHUNK),
                            n_x_tiles=nxt)

    return pl.pallas_call(
        kernel,
        grid=(n_rows // tm,),
        in_specs=[row, pl.BlockSpec((None, d, d), lambda i: (layer, 0, 0)), *x_specs, _vec_spec(d), _vec_spec(d),
                  rows.mod_spec(layer, 2, d), rows.mod_spec(layer, 3, d), rows.mod_spec(layer, 4, d)],
        out_specs=[row, row],
        out_shape=[jax.ShapeDtypeStruct((n_rows, d), f32), jax.ShapeDtypeStruct((n_rows, d), bf16)],
        compiler_params=_cparams("parallel"),
        name="outproj",
    )(merged, w_out, *xs, g1, g2, mods, mods, mods)


def _ffn_kernel(*refs, n_f, n_tiles, has_next, row_chunk):
    if has_next:
        (h_ref, wa_ref, wb_ref, w2_ref, x_hbm, g3_ref, gate_ref, gn_ref, sh_ref, sc_ref,
         xo_hbm, ho_hbm, acc, xbuf, hbuf, sems) = refs
    else:
        h_ref, wa_ref, wb_ref, w2_ref, x_hbm, g3_ref, gate_ref, xo_hbm, acc, xbuf, sems = refs
    i, f = pl.program_id(0), pl.program_id(1)
    tm = h_ref.shape[0]

    def tile_rows(t):
        return pl.ds(pl.multiple_of(t * tm, tm), tm)

    def x_in(t):
        return pltpu.make_async_copy(x_hbm.at[tile_rows(t), :], xbuf, sems.at[0])

    def outs(t):
        copies = [pltpu.make_async_copy(xbuf, xo_hbm.at[tile_rows(t), :], sems.at[1])]
        if has_next:
            copies.append(pltpu.make_async_copy(hbuf, ho_hbm.at[tile_rows(t), :], sems.at[2]))
        return copies

    @pl.when(f == 1)
    def _():
        @pl.when(i > 0)
        def _():
            for cp in outs(i - 1):
                cp.wait()
        x_in(i).start()

    def step(first, last):
        if last:
            x_in(i).wait()
            res_scale = gate_ref[...] * g3_ref[...]
            if has_next:
                mod_scale = gn_ref[...] * (1.0 + sc_ref[...])
        for r in range(tm // row_chunk):
            rows = slice(r * row_chunk, (r + 1) * row_chunk)
            h = h_ref[rows, :]
            a = jnp.dot(h, wa_ref[...], preferred_element_type=f32)
            b = jnp.dot(h, wb_ref[...], preferred_element_type=f32)
            p = jnp.dot((jax.nn.silu(a) * b).astype(bf16), w2_ref[...], preferred_element_type=f32)
            if not first:
                p = acc[rows, :] + p
            if not last:
                acc[rows, :] = p
            else:
                xn = xbuf[rows, :] + _rms(p) * res_scale
                xbuf[rows, :] = xn
                if has_next:
                    hbuf[rows, :] = (_rms(xn) * mod_scale + sh_ref[...]).astype(bf16)
        if last:
            for cp in outs(i):
                cp.start()

            @pl.when(i == n_tiles - 1)
            def _():
                for cp in outs(i):
                    cp.wait()

    pl.when(f == 0)(lambda: step(True, False))
    pl.when(f == n_f - 1)(lambda: step(False, True))
    pl.when((f > 0) & (f < n_f - 1))(lambda: step(False, False))


def _ffn(hx, w1, w2, xs, g3, g_next, mods, layer, rows, tf, n_rows):
    d = xs.shape[1]
    n_f = w2.shape[1] // tf
    assert n_f >= 3
    tm = rows.tm
    n_tiles = n_rows // tm
    has_next = g_next is not None
    row = pl.BlockSpec((tm, d), lambda i, f: (i, 0))
    hbm = pl.BlockSpec(memory_space=pl.ANY)
    in_specs = [row,
                pl.BlockSpec((None, d, tf), lambda i, f: (layer, 0, f)),
                pl.BlockSpec((None, d, tf), lambda i, f: (layer, 0, n_f + f)),
                pl.BlockSpec((None, tf, d), lambda i, f: (layer, f, 0)),
                hbm, _vec_spec(d), rows.mod_spec(layer, 5, d)]
    args = [hx, w1, w1, w2, xs, g3, mods]
    out_specs = [hbm]
    out_shape = [jax.ShapeDtypeStruct((n_rows, d), f32)]
    scratch = [pltpu.VMEM((tm, d), f32), pltpu.VMEM((tm, d), f32)]
    if has_next:
        in_specs += [_vec_spec(d), rows.mod_spec(layer + 1, 0, d), rows.mod_spec(layer + 1, 1, d)]
        args += [g_next, mods, mods]
        out_specs.append(hbm)
        out_shape.append(jax.ShapeDtypeStruct((n_rows, d), bf16))
        scratch.append(pltpu.VMEM((tm, d), bf16))
    scratch.append(pltpu.SemaphoreType.DMA((3,)))
    out = pl.pallas_call(
        functools.partial(_ffn_kernel, n_f=n_f, n_tiles=n_tiles, has_next=has_next,
                          row_chunk=_tile(tm, FFN_ROW_CHUNK)),
        grid=(n_tiles, n_f),
        in_specs=in_specs,
        out_specs=out_specs,
        out_shape=out_shape,
        scratch_shapes=scratch,
        compiler_params=_cparams("arbitrary", "arbitrary"),
        name="ffn",
    )(*args)
    return out if has_next else (out[0], None)


def kernel(x, c, ctx, c_ctx, ada_w, ada_b, norm_g, w_in, ret_log_decay, conv_w, sgu_w, sgu_b,
           w_branch, w_out, ffn_w_in, ffn_w_out):
    batch, seq, d = x.shape
    ctx_len = ctx.shape[1]
    depth = ada_w.shape[0]
    d_ff = ffn_w_out.shape[1]
    assert batch + 1 <= MOD_ROWS and w_in.shape[2] == N_MIX_COLS + N_BRANCH * d

    rows = _Rows(batch, seq, ctx_len, _tile(batch * ctx_len, 512))
    tm_big = _tile(batch * ctx_len, 1024)
    rows_big = _Rows(batch, seq, ctx_len, tm_big)

    cc = jnp.zeros((MOD_ROWS, d), f32).at[:batch].set(c).at[batch].set(c_ctx)
    mods = _adaln(cc, ada_w, ada_b).reshape(depth * MOD_ROWS * 6, 1, d)

    rope_tabs = _rope_tables(seq)
    dft_x = _dft_tables(seq)
    dft_c = _dft_tables(ctx_len)

    xs = (x.reshape(batch * seq, d), ctx.reshape(batch * ctx_len, d))
    hx = _norm_mod(*xs, norm_g[0, 0:1], mods, rows)

    w_in_b, w_branch_b, w_out_b = w_in.astype(bf16), w_branch.astype(bf16), w_out.astype(bf16)
    ffn_w_in_b, ffn_w_out_b, sgu_w_b = ffn_w_in.astype(bf16), ffn_w_out.astype(bf16), sgu_w.astype(bf16)
    tn = _tile(d, 1024)
    assert N_MIX_COLS % tn == 0

    for l in range(depth):
        sgu_bias = jnp.repeat(jnp.transpose(sgu_b[l]), GROUP_DIM, axis=1)

        last = l + 1 == depth
        n_rows = rows.m_x if last else rows.m

        proj = _matmul(hx, w_in_b, l, N_MIX_COLS, tm_big, N_MIX_COLS // 2, "inproj")
        ya = _retention(proj, ret_log_decay[l], rows, rope_tabs, True)
        yb = _fourier(proj, rows, dft_x, True)
        if last:
            ya_c, yb_c = ya, yb
        else:
            ya_c = _retention(proj, ret_log_decay[l], rows, None, False)
            yb_c = _fourier(proj, rows, dft_c, False)
        yc, yd = _local_mix(proj, sgu_w_b, l, sgu_bias, conv_w, rows, rows.tm, n_rows)
        merged = _merge(hx, (ya, ya_c, yb, yb_c, yc, yd), w_in_b, w_branch_b, l, rows_big, tn, n_rows)
        xs, hx2 = _outproj(merged, w_out_b, xs, norm_g[l, 1:2], norm_g[l, 2:3], mods, l, rows, n_rows)
        g_next = None if last else norm_g[l + 1, 0:1]
        xs, hx = _ffn(hx2, ffn_w_in_b, ffn_w_out_b, xs, norm_g[l, 3:4], g_next, mods, l, rows_big,
                      _tile(d_ff, 512), n_rows)

    return xs.reshape(batch, seq, d)
```

```python
import functools

import jax
import jax.numpy as jnp
from jax import lax
from jax.experimental import pallas as pl
from jax.experimental.pallas import tpu as pltpu

f32 = jnp.float32
bf16 = jnp.bfloat16

GRID_W = 64
MIX_W = 512
N_BRANCH = 4
GROUP_DIM = 128
N_GROUPS = MIX_W // GROUP_DIM
CHUNK = 128
ROPE_BASE = 10000.0
EPS = 1e-6
N_MIX_COLS = 10 * MIX_W
COL_F, COL_U, COL_V, COL_SB, COL_SC, COL_SX = 4, 5, 6, 7, 8, 9
MOD_ROWS = 16
CHUNK_UNROLL = 32
FFN_ROW_CHUNK = 512
MERGE_ROW_CHUNK = 256
OUTPROJ_ROW_CHUNK = 256

VMEM_LIMIT_BYTES = 56 * 1024 * 1024


def _cparams(*sem):
    return pltpu.CompilerParams(dimension_semantics=sem, vmem_limit_bytes=VMEM_LIMIT_BYTES)


def _tile(n, pref):
    t = min(n, pref)
    while n % t:
        t //= 2
    return t


def _row_chunks(n, chunk, split_tail=False):
    chunk = _tile(n, chunk)
    bounds = list(range(0, n + 1, chunk))
    if split_tail and chunk % 256 == 0:
        bounds.insert(-1, n - chunk // 2)
    return [slice(a, b) for a, b in zip(bounds[:-1], bounds[1:])]


def _rms(x):
    return x * lax.rsqrt(jnp.mean(x * x, axis=-1, keepdims=True) + EPS)


def _adaln_kernel(c_ref, w_ref, b_ref, o_ref):
    s = jax.nn.silu(c_ref[...]).astype(bf16)
    o_ref[...] = jnp.dot(s, w_ref[...].astype(bf16), preferred_element_type=f32) + b_ref[...]


def _adaln(cc, ada_w, ada_b):
    depth, d, n6 = ada_w.shape
    tn = _tile(n6, 1024)
    return pl.pallas_call(
        _adaln_kernel,
        grid=(depth, n6 // tn),
        in_specs=[
            pl.BlockSpec((MOD_ROWS, d), lambda l, j: (0, 0)),
            pl.BlockSpec((None, d, tn), lambda l, j: (l, 0, j)),
            pl.BlockSpec((None, 1, tn), lambda l, j: (l, 0, j)),
        ],
        out_specs=pl.BlockSpec((None, MOD_ROWS, tn), lambda l, j: (l, 0, j)),
        out_shape=jax.ShapeDtypeStruct((depth, MOD_ROWS, n6), f32),
        compiler_params=_cparams("parallel", "parallel"),
        name="adaln",
    )(cc, ada_w, ada_b.reshape(depth, 1, n6))


class _Rows:
    def __init__(self, batch, seq, ctx_len, tm):
        self.batch, self.seq, self.ctx_len, self.tm = batch, seq, ctx_len, tm
        self.m_x, self.m_c = batch * seq, batch * ctx_len
        self.m = self.m_x + self.m_c
        assert seq % tm == 0 and self.m_c % tm == 0
        self.n_x_tiles = self.m_x // tm
        self.n_tiles = self.m // tm
        self.per_batch = seq // tm

    def mod_spec(self, layer, k, d):
        def index(i, *_):
            r = jnp.where(i < self.n_x_tiles, i // self.per_batch, self.batch)
            return ((layer * MOD_ROWS + r) * 6 + k, 0, 0)
        return pl.BlockSpec((None, 1, d), index)


def _vec_spec(d):
    return pl.BlockSpec((1, d), lambda i, *_: (0, 0))


def _norm_mod_kernel(x_ref, c_ref, g_ref, sh_ref, sc_ref, hx_ref, *, n_x_tiles):
    def emit(x):
        hx_ref[...] = (_rms(x) * (g_ref[...] * (1.0 + sc_ref[...])) + sh_ref[...]).astype(bf16)

    @pl.when(pl.program_id(0) < n_x_tiles)
    def _():
        emit(x_ref[...])

    @pl.when(pl.program_id(0) >= n_x_tiles)
    def _():
        emit(c_ref[...])


def _norm_mod(x2d, c2d, g, mods, rows):
    d = x2d.shape[1]
    tm, nxt = rows.tm, rows.n_x_tiles
    return pl.pallas_call(
        functools.partial(_norm_mod_kernel, n_x_tiles=nxt),
        grid=(rows.n_tiles,),
        in_specs=[
            pl.BlockSpec((tm, d), lambda i: (jnp.minimum(i, nxt - 1), 0)),
            pl.BlockSpec((tm, d), lambda i: (jnp.maximum(i - nxt, 0), 0)),
            _vec_spec(d),
            rows.mod_spec(0, 0, d),
            rows.mod_spec(0, 1, d),
        ],
        out_specs=pl.BlockSpec((tm, d), lambda i: (i, 0)),
        out_shape=jax.ShapeDtypeStruct((rows.m, d), bf16),
        compiler_params=_cparams("parallel"),
        name="norm_mod",
    )(x2d, c2d, g, mods, mods)


def _matmul_kernel(a_ref, b_ref, o_ref):
    o_ref[...] = jnp.dot(a_ref[...], b_ref[...], preferred_element_type=f32).astype(o_ref.dtype)


def _matmul(a, b_stack, layer, n, tm, tn, name):
    m, k = a.shape
    return pl.pallas_call(
        _matmul_kernel,
        grid=(m // tm, n // tn),
        in_specs=[pl.BlockSpec((tm, k), lambda i, j: (i, 0)),
                  pl.BlockSpec((None, k, tn), lambda i, j: (layer, 0, j))],
        out_specs=pl.BlockSpec((tm, tn), lambda i, j: (i, j)),
        out_shape=jax.ShapeDtypeStruct((m, n), bf16),
        compiler_params=_cparams("parallel", "arbitrary"),
        name=name,
    )(a, b_stack)


def _rope_tables(seq):
    t = jnp.arange(seq)
    n = GROUP_DIM // 2
    freqs = ROPE_BASE ** (-jnp.arange(0, n, 2, dtype=f32) / n)

    def tab(pos):
        ang = pos.astype(f32)[:, None] * freqs[None, :]
        c, s = jnp.cos(ang), jnp.sin(ang)
        return jnp.concatenate([c, c], -1), jnp.concatenate([-s, s], -1)

    c1, s1 = tab(t // GRID_W)
    c2, s2 = tab(t % GRID_W)
    dst = jnp.arange(GROUP_DIM)
    src = jnp.where((dst % n) < n // 2, dst + n // 2, dst - n // 2)
    perm = (jnp.arange(GROUP_DIM)[:, None] == src[None, :]).astype(bf16)
    return jnp.concatenate([c1, c2], -1), jnp.concatenate([s1, s2], -1), perm


def _retention_kernel(*refs, n_chunks, rope, ctx_len):
    refs = list(refs)
    lg_ref, q_ref, k_ref, v_ref, g_ref = refs[:5]
    pos = 5
    if ctx_len:
        kc_ref, vc_ref = refs[pos:pos + 2]
        pos += 2
    if rope:
        cos_ref, sin_ref, perm_ref = refs[pos:pos + 3]
        pos += 3
    o_ref, qs, ks, sf, sb = refs[pos:]

    c_len, dh = CHUNK, GROUP_DIM
    h = pl.program_id(1)
    lgf, lgb = lg_ref[0, h], lg_ref[1, h]
    scale = dh ** -0.5
    ri = lax.broadcasted_iota(jnp.int32, (c_len, dh), 0).astype(f32)
    ci = lax.broadcasted_iota(jnp.int32, (c_len, dh), 1).astype(f32)
    rel = ri - ci
    decay = (jnp.where(rel >= 0, jnp.exp(lgf * jnp.maximum(rel, 0.0)), 0.0)
             + jnp.where(rel <= 0, jnp.exp(lgb * jnp.maximum(-rel, 0.0)), 0.0))
    kw_f = jnp.exp(lgf * (c_len - 1 - ri))
    kw_b = jnp.exp(lgb * ri)
    qw_f = jnp.exp(lgf * (ri + 1.0))
    qw_b = jnp.exp(lgb * (c_len - ri))
    cd_f = jnp.exp(jnp.full((dh, dh), lgf * c_len, f32))
    cd_b = jnp.exp(jnp.full((dh, dh), lgb * c_len, f32))

    def rows_of(c):
        return pl.ds(pl.multiple_of(c * c_len, c_len), c_len)

    def kt_dot(kb, vw):
        return lax.dot_general(kb, vw, (((0,), (0,)), ((), ())), preferred_element_type=f32)

    def prep(c, carry):
        rows = rows_of(c)
        q, k = q_ref[rows, :], k_ref[rows, :]
        qf, kf = q.astype(f32), k.astype(f32)
        if rope:
            cos, sin = cos_ref[rows, :], sin_ref[rows, :]
            qf = qf * cos + jnp.dot(q, perm_ref[...], preferred_element_type=f32) * sin
            kf = kf * cos + jnp.dot(k, perm_ref[...], preferred_element_type=f32) * sin
        kb = (kf * scale).astype(bf16)
        qs[rows, :] = qf.astype(bf16)
        ks[rows, :] = kb
        v = v_ref[rows, :].astype(f32)
        vw = jnp.concatenate([v * kw_f, v * kw_b], axis=1).astype(bf16)
        ds = kt_dot(kb, vw)
        sf[c] = ds[:, :dh]
        sb[c] = ds[:, dh:]
        return carry

    unroll = min(n_chunks, CHUNK_UNROLL)
    lax.fori_loop(0, n_chunks, prep, 0, unroll=unroll)

    if ctx_len:
        mi = lax.broadcasted_iota(jnp.int32, (ctx_len, dh), 0).astype(f32)
        kcb = (kc_ref[...].astype(f32) * scale).astype(bf16)
        vc = vc_ref[...].astype(f32)
        vcw = jnp.concatenate([vc * jnp.exp(lgf * (ctx_len - 1 - mi)), vc * jnp.exp(lgb * mi)],
                              axis=1).astype(bf16)
        s0 = kt_dot(kcb, vcw)
        s0_f, s0_b = s0[:, :dh], s0[:, dh:]
    else:
        s0_f = s0_b = jnp.zeros((dh, dh), f32)

    def scan_f(c, s):
        nxt = cd_f * s + sf[c]
        sf[c] = s
        return nxt

    def scan_b(j, s):
        c = n_chunks - 1 - j
        nxt = cd_b * s + sb[c]
        sb[c] = s
        return nxt

    lax.fori_loop(0, n_chunks, scan_f, s0_f)
    lax.fori_loop(0, n_chunks, scan_b, s0_b)

    def emit(c, carry):
        rows = rows_of(c)
        q, k, v = qs[rows, :], ks[rows, :], v_ref[rows, :]
        sc = lax.dot_general(q, k, (((1,), (1,)), ((), ())), preferred_element_type=f32) * decay
        inner = jnp.dot(sc.astype(bf16), v, preferred_element_type=f32)
        s2 = jnp.concatenate([sf[c], sb[c]], axis=1).astype(bf16)
        cr = jnp.dot(q, s2, preferred_element_type=f32)
        ret = inner + qw_f * cr[:, :dh] + qw_b * cr[:, dh:]
        g = g_ref[rows, :].astype(f32)
        o_ref[rows, :] = (jax.nn.silu(g) * _rms(ret)).astype(bf16)
        return carry

    lax.fori_loop(0, n_chunks, emit, 0, unroll=unroll)


def _retention(proj, log_decay, rows, rope_tabs, latent):
    seq = rows.seq if latent else rows.ctx_len
    n_chunks = seq // CHUNK
    row0 = 0 if latent else rows.m_x // seq
    hq = MIX_W // GROUP_DIM

    def col(off):
        return pl.BlockSpec((seq, GROUP_DIM), lambda b, h: (row0 + b, off * hq + h))

    in_specs = [pl.BlockSpec(memory_space=pltpu.SMEM), col(0), col(1), col(2), col(3)]
    args = [log_decay, proj, proj, proj, proj]
    if latent:
        cb0 = rows.m_x // rows.ctx_len
        for off in (1, 2):
            in_specs.append(pl.BlockSpec((rows.ctx_len, GROUP_DIM),
                                         lambda b, h, off=off: (cb0 + b, off * hq + h)))
            args.append(proj)
        cos, sin, perm = rope_tabs
        in_specs += [pl.BlockSpec((seq, GROUP_DIM), lambda b, h: (0, 0)),
                     pl.BlockSpec((seq, GROUP_DIM), lambda b, h: (0, 0)),
                     pl.BlockSpec((GROUP_DIM, GROUP_DIM), lambda b, h: (0, 0))]
        args += [cos, sin, perm]

    return pl.pallas_call(
        functools.partial(_retention_kernel, n_chunks=n_chunks, rope=latent,
                          ctx_len=rows.ctx_len if latent else 0),
        grid=(rows.batch, hq),
        in_specs=in_specs,
        out_specs=pl.BlockSpec((seq, GROUP_DIM), lambda b, h: (b, h)),
        out_shape=jax.ShapeDtypeStruct((rows.batch * seq, MIX_W), bf16),
        scratch_shapes=[pltpu.VMEM((seq, GROUP_DIM), bf16), pltpu.VMEM((seq, GROUP_DIM), bf16),
                        pltpu.VMEM((n_chunks, GROUP_DIM, GROUP_DIM), f32),
                        pltpu.VMEM((n_chunks, GROUP_DIM, GROUP_DIM), f32)],
        compiler_params=_cparams("parallel", "parallel"),
        name="retention_x" if latent else "retention_ctx",
    )(*args)


FOURIER_TILE = 512
FOURIER_EXTRA = 16


def _dft_tables(seq):
    def phase(k, n):
        ang = (k % n).astype(f32) * (2.0 * jnp.pi / n)
        return jnp.cos(ang), jnp.sin(ang)

    def cs(n_f, n):
        r = 1 << ((n_f.bit_length() - 1) // 2)
        t = jnp.arange(n, dtype=jnp.int32)
        ca, sa = phase((r * jnp.arange(n_f // r, dtype=jnp.int32))[:, None] * t[None, :], n)
        cb, sb = phase(jnp.arange(r, dtype=jnp.int32)[:, None] * t[None, :], n)
        cos = ca[:, None, :] * cb[None] - sa[:, None, :] * sb[None]
        sin = sa[:, None, :] * cb[None] + ca[:, None, :] * sb[None]
        return cos.reshape(n_f, n), sin.reshape(n_f, n)

    half = seq // 2
    tm = _tile(half, FOURIER_TILE)
    n_tiles = half // tm
    cl, sl = cs(half, seq)
    nyq = jnp.where(jnp.arange(seq) % 2 == 0, 1.0, -1.0).astype(f32)
    extra = jnp.zeros((n_tiles, FOURIER_EXTRA, seq), f32).at[:, 0, :].set(nyq)
    pos = jnp.concatenate([cl.reshape(n_tiles, tm, seq), sl.reshape(n_tiles, tm, seq), extra], axis=1)

    cc, sc = cs(GROUP_DIM, GROUP_DIM)
    norm = (seq * GROUP_DIM) ** -0.5
    mix = jnp.concatenate([jnp.concatenate([cc, cc], 1), jnp.concatenate([-sc, sc], 1)], 0) * norm

    s = jnp.arange(tm)
    flip = ((s[:, None] + s[None, :] == tm) & (s[:, None] >= 1)).astype(bf16)
    return pos.astype(bf16), mix.astype(bf16), flip


def _fourier_kernel(z_ref, pos_ref, mix_ref, flip_ref, o_ref, stash, *, seq, tm, n_tiles):
    j = pl.program_id(1)
    m = n_tiles - 1 - j

    def mixed(cz, sz):
        qs, ps = [], []
        for g in range(N_GROUPS):
            cols = slice(g * GROUP_DIM, (g + 1) * GROUP_DIM)
            szg = jnp.zeros_like(cz[:, cols]) if sz is None else sz[:, cols]
            lhs = jnp.concatenate([cz[:, cols], szg], axis=1).astype(bf16)
            out = jnp.dot(lhs, mix_ref[...], preferred_element_type=f32)
            qs.append(out[:, :GROUP_DIM])
            ps.append(out[:, GROUP_DIM:])
        return jnp.concatenate(qs, axis=1), jnp.concatenate(ps, axis=1)

    r = jnp.dot(pos_ref[...], z_ref[...], preferred_element_type=f32)
    q, p = mixed(r[:tm], r[tm:2 * tm])

    @pl.when(j == 0)
    def _():
        stash[...] = mixed(r[2 * tm:], None)[1]

    above = stash[0:1, :]
    o_ref[pl.ds(pl.multiple_of(m * tm, tm), tm), :] = q.astype(bf16)
    mir = jnp.dot(flip_ref[...], p.astype(bf16), preferred_element_type=f32)
    row = lax.broadcasted_iota(jnp.int32, mir.shape, 0)
    mir = jnp.where(row == 0, above, mir)
    o_ref[pl.ds(pl.multiple_of(seq // 2 + j * tm, tm), tm), :] = mir.astype(bf16)
    stash[...] = p[:FOURIER_EXTRA]


def _fourier(proj, rows, tables, latent):
    seq = rows.seq if latent else rows.ctx_len
    pos_tab, mix_tab, flip_tab = tables
    n_tiles, tab_rows, _ = pos_tab.shape
    tm = flip_tab.shape[0]
    row0 = 0 if latent else rows.m_x // seq
    in_specs = [pl.BlockSpec((seq, MIX_W), lambda b, j: (row0 + b, COL_F)),
                pl.BlockSpec((None, tab_rows, seq), lambda b, j: (n_tiles - 1 - j, 0, 0)),
                pl.BlockSpec((2 * GROUP_DIM, 2 * GROUP_DIM), lambda b, j: (0, 0)),
                pl.BlockSpec((tm, tm), lambda b, j: (0, 0))]
    return pl.pallas_call(
        functools.partial(_fourier_kernel, seq=seq, tm=tm, n_tiles=n_tiles),
        grid=(rows.batch, n_tiles),
        in_specs=in_specs,
        out_specs=pl.BlockSpec((seq, MIX_W), lambda b, j: (b, 0)),
        out_shape=jax.ShapeDtypeStruct((rows.batch * seq, MIX_W), bf16),
        scratch_shapes=[pltpu.VMEM((FOURIER_EXTRA, MIX_W), f32)],
        compiler_params=_cparams("parallel", "arbitrary"),
        name="fourier_x" if latent else "fourier_ctx",
    )(proj, pos_tab, mix_tab, flip_tab)


def _sgu_kernel(u_ref, v_ref, w_ref, b_ref, o_ref, *, tm):
    for ch in range(tm // CHUNK):
        rows = slice(ch * CHUNK, (ch + 1) * CHUNK)
        u = jax.nn.gelu(u_ref[rows, :].astype(f32))
        v = jax.nn.gelu(v_ref[rows, :].astype(f32))
        outs = []
        for g in range(N_GROUPS):
            cols = slice(g * GROUP_DIM, (g + 1) * GROUP_DIM)
            vg = v[:, cols]
            dv = vg - jnp.mean(vg, axis=-1, keepdims=True)
            vn = dv * lax.rsqrt(jnp.mean(dv * dv, axis=-1, keepdims=True) + EPS)
            s = jnp.dot(w_ref[g], vn.astype(bf16), preferred_element_type=f32) + b_ref[:, cols]
            outs.append(u[:, cols] * s)
        o_ref[rows, :] = jnp.concatenate(outs, axis=1).astype(bf16)


def _conv_kernel(b_ref, c_ref, x_ref, w_ref, o_ref, *, tm, n_x_tiles, ctx_len):
    y = c_ref[...].astype(f32) * x_ref[...].astype(f32)
    r = lax.broadcasted_iota(jnp.int32, y.shape, 0)
    period = jnp.where(pl.program_id(0) < n_x_tiles, GRID_W, ctx_len)
    rp = r & (period - 1)
    prev = jnp.where(rp == 0, 0.0, pltpu.roll(y, 1, axis=0))
    nxt = jnp.where(rp == period - 1, 0.0, pltpu.roll(y, tm - 1, axis=0))
    conv = w_ref[0:1, :] * prev + w_ref[1:2, :] * y + w_ref[2:3, :] * nxt
    o_ref[...] = (b_ref[...].astype(f32) * conv).astype(bf16)


def _local_mix_kernel(u_ref, v_ref, ws_ref, bs_ref, b_ref, c_ref, x_ref, wc_ref, oc_ref, od_ref,
                      *, tm, n_x_tiles, ctx_len):
    _sgu_kernel(u_ref, v_ref, ws_ref, bs_ref, oc_ref, tm=tm)
    _conv_kernel(b_ref, c_ref, x_ref, wc_ref, od_ref, tm=tm, n_x_tiles=n_x_tiles, ctx_len=ctx_len)


def _local_mix(proj, sgu_w, layer, sgu_bias, conv_w, rows, tm, n_rows):
    assert GRID_W & (GRID_W - 1) == 0 and rows.ctx_len & (rows.ctx_len - 1) == 0
    assert tm % GRID_W == 0 and tm % rows.ctx_len == 0 and tm % CHUNK == 0

    def col(c):
        return pl.BlockSpec((tm, MIX_W), lambda i: (i, c))

    out = jax.ShapeDtypeStruct((n_rows, MIX_W), bf16)
    return pl.pallas_call(
        functools.partial(_local_mix_kernel, tm=tm, n_x_tiles=rows.m_x // tm, ctx_len=rows.ctx_len),
        grid=(n_rows // tm,),
        in_specs=[col(COL_U), col(COL_V),
                  pl.BlockSpec((None, N_GROUPS, CHUNK, CHUNK), lambda i: (layer, 0, 0, 0)),
                  pl.BlockSpec((CHUNK, MIX_W), lambda i: (0, 0)),
                  col(COL_SB), col(COL_SC), col(COL_SX),
                  pl.BlockSpec((None, 3, MIX_W), lambda i: (layer, 0, 0))],
        out_specs=[col(0), col(0)],
        out_shape=[out, out],
        compiler_params=_cparams("parallel"),
        name="local_mix",
    )(proj, proj, sgu_w, sgu_bias, proj, proj, proj, conv_w)


def _merge_kernel(hx_ref, yax_ref, yac_ref, ybx_ref, ybc_ref, yc_ref, yd_ref, wg0_ref, wg1_ref, wb0_ref, wb1_ref,
                  o_ref, acc, *, row_chunk, n_x_tiles):
    pair = pl.program_id(2)
    latent = pl.program_id(0) < n_x_tiles

    def term(rows, y, wg_ref, wb_ref):
        gate = jax.nn.sigmoid(jnp.dot(hx_ref[rows, :], wg_ref[...], preferred_element_type=f32))
        return gate * jnp.dot(y, wb_ref[...], preferred_element_type=f32)

    chunks = _row_chunks(hx_ref.shape[0], row_chunk)

    @pl.when(pair == 0)
    def _():
        for rows in chunks:
            ya = jnp.where(latent, yax_ref[rows, :], yac_ref[rows, :])
            yb = jnp.where(latent, ybx_ref[rows, :], ybc_ref[rows, :])
            acc[rows, :] = term(rows, ya, wg0_ref, wb0_ref) + term(rows, yb, wg1_ref, wb1_ref)

    @pl.when(pair == 1)
    def _():
        for rows in chunks:
            t = acc[rows, :] + term(rows, yc_ref[rows, :], wg0_ref, wb0_ref)
            o_ref[rows, :] = (t + term(rows, yd_ref[rows, :], wg1_ref, wb1_ref)).astype(bf16)


def _merge(hx, ys, w_in, w_branch, layer, rows, tn, n_rows):
    assert N_BRANCH == 4
    d = hx.shape[1]
    m = n_rows
    tm = rows.tm
    nxt = rows.n_x_tiles
    nj = d // tn
    gate0 = N_MIX_COLS // tn
    y_spec = pl.BlockSpec((tm, MIX_W), lambda i, j, p: (i, 0))
    yx_spec = pl.BlockSpec((tm, MIX_W), lambda i, j, p: (jnp.minimum(i, nxt - 1), 0))
    yc_spec = pl.BlockSpec((tm, MIX_W), lambda i, j, p: (jnp.maximum(i - nxt, 0), 0))

    def wg_spec(k):
        return pl.BlockSpec((None, d, tn), lambda i, j, p: (layer, 0, gate0 + (2 * p + k) * nj + j))

    def wb_spec(k):
        return pl.BlockSpec((None, None, MIX_W, tn), lambda i, j, p: (layer, 2 * p + k, 0, j))

    return pl.pallas_call(
        functools.partial(_merge_kernel, row_chunk=_tile(tm, MERGE_ROW_CHUNK), n_x_tiles=nxt),
        grid=(m // tm, nj, N_BRANCH // 2),
        in_specs=[pl.BlockSpec((tm, d), lambda i, j, p: (i, 0)), yx_spec, yc_spec, yx_spec, yc_spec, y_spec, y_spec,
                  wg_spec(0), wg_spec(1), wb_spec(0), wb_spec(1)],
        out_specs=pl.BlockSpec((tm, tn), lambda i, j, p: (i, j)),
        out_shape=jax.ShapeDtypeStruct((m, d), bf16),
        scratch_shapes=[pltpu.VMEM((tm, tn), f32)],
        compiler_params=_cparams("parallel", "arbitrary", "arbitrary"),
        name="merge",
    )(hx, *ys, w_in, w_in, w_branch, w_branch)


def _outproj_kernel(m_ref, w_ref, xl_ref, xc_ref, g1_ref, g2_ref, gate_ref, sh_ref, sc_ref, xo_ref, ho_ref,
                    *, row_chunk, n_x_tiles):
    latent = pl.program_id(0) < n_x_tiles
    res_scale = gate_ref[...] * g1_ref[...]
    mod_scale = g2_ref[...] * (1.0 + sc_ref[...])
    for rows in _row_chunks(m_ref.shape[0], row_chunk, split_tail=True):
        mix = jnp.dot(m_ref[rows, :], w_ref[...], preferred_element_type=f32)
        x = xl_ref[rows, :] if xc_ref is None else jnp.where(latent, xl_ref[rows, :], xc_ref[rows, :])
        xn = x + _rms(mix) * res_scale
        xo_ref[rows, :] = xn
        ho_ref[rows, :] = (_rms(xn) * mod_scale + sh_ref[...]).astype(bf16)


def _outproj(merged, w_out, xs, g1, g2, mods, layer, rows, n_rows):
    d = merged.shape[1]
    tm, nxt = rows.tm, rows.n_x_tiles
    row = pl.BlockSpec((tm, d), lambda i: (i, 0))
    if isinstance(xs, tuple):
        x_specs = [pl.BlockSpec((tm, d), lambda i: (jnp.minimum(i, nxt - 1), 0)),
                   pl.BlockSpec((tm, d), lambda i: (jnp.maximum(i - nxt, 0), 0))]
        kernel = functools.partial(_outproj_kernel, row_chunk=OUTPROJ_ROW_CHUNK, n_x_tiles=nxt)
    else:
        xs, x_specs = (xs,), [row]

        def kernel(m_ref, w_ref, x_ref, *rest):
            _outproj_kernel(m_ref, w_ref, x_ref, None, *rest, row_chunk=OUTPROJ_ROW_CHUNK, n_x_tiles=nxt)

    return pl.pallas_call(
        kernel,
        grid=(n_rows // tm,),
        in_specs=[row, pl.BlockSpec((None, d, d), lambda i: (layer, 0, 0), pipeline_mode=pl.Buffered(1)),
                  *x_specs, _vec_spec(d), _vec_spec(d),
                  rows.mod_spec(layer, 2, d), rows.mod_spec(layer, 3, d), rows.mod_spec(layer, 4, d)],
        out_specs=[row, row],
        out_shape=[jax.ShapeDtypeStruct((n_rows, d), f32), jax.ShapeDtypeStruct((n_rows, d), bf16)],
        compiler_params=_cparams("parallel"),
        name="outproj",
    )(merged, w_out, *xs, g1, g2, mods, mods, mods)


def _ffn_kernel(*refs, n_f, n_tiles, has_next, row_chunk):
    if has_next:
        (h_ref, wa_ref, wb_ref, w2_ref, x_hbm, g3_ref, gate_ref, gn_ref, sh_ref, sc_ref,
         xo_hbm, ho_hbm, acc, xbuf, hbuf, sems) = refs
    else:
        h_ref, wa_ref, wb_ref, w2_ref, x_hbm, g3_ref, gate_ref, xo_hbm, acc, xbuf, sems = refs
    i, f = pl.program_id(0), pl.program_id(1)
    tm = h_ref.shape[0]

    def tile_rows(t):
        return pl.ds(pl.multiple_of(t * tm, tm), tm)

    def x_in(t):
        return pltpu.make_async_copy(x_hbm.at[tile_rows(t), :], xbuf, sems.at[0])

    def outs(t):
        copies = [pltpu.make_async_copy(xbuf, xo_hbm.at[tile_rows(t), :], sems.at[1])]
        if has_next:
            copies.append(pltpu.make_async_copy(hbuf, ho_hbm.at[tile_rows(t), :], sems.at[2]))
        return copies

    @pl.when(f == 1)
    def _():
        @pl.when(i > 0)
        def _():
            for cp in outs(i - 1):
                cp.wait()
        x_in(i).start()

    def step(first, last):
        if last:
            x_in(i).wait()
            res_scale = gate_ref[...] * g3_ref[...]
            if has_next:
                mod_scale = gn_ref[...] * (1.0 + sc_ref[...])
        for rows in _row_chunks(tm, row_chunk, split_tail=last):
            h = h_ref[rows, :]
            a = jnp.dot(h, wa_ref[...], preferred_element_type=f32)
            b = jnp.dot(h, wb_ref[...], preferred_element_type=f32)
            p = jnp.dot((jax.nn.silu(a) * b).astype(bf16), w2_ref[...], preferred_element_type=f32)
            if not first:
                p = acc[rows, :] + p
            if not last:
                acc[rows, :] = p
            else:
                xn = xbuf[rows, :] + _rms(p) * res_scale
                xbuf[rows, :] = xn
                if has_next:
                    hbuf[rows, :] = (_rms(xn) * mod_scale + sh_ref[...]).astype(bf16)
        if last:
            for cp in outs(i):
                cp.start()

            @pl.when(i == n_tiles - 1)
            def _():
                for cp in outs(i):
                    cp.wait()

    pl.when(f == 0)(lambda: step(True, False))
    pl.when(f == n_f - 1)(lambda: step(False, True))
    pl.when((f > 0) & (f < n_f - 1))(lambda: step(False, False))


def _ffn(hx, w1, w2, xs, g3, g_next, mods, layer, rows, tf, n_rows):
    d = xs.shape[1]
    n_f = w2.shape[1] // tf
    assert n_f >= 3
    tm = rows.tm
    n_tiles = n_rows // tm
    has_next = g_next is not None
    row = pl.BlockSpec((tm, d), lambda i, f: (i, 0))
    hbm = pl.BlockSpec(memory_space=pl.ANY)
    in_specs = [row,
                pl.BlockSpec((None, d, tf), lambda i, f: (layer, 0, f)),
                pl.BlockSpec((None, d, tf), lambda i, f: (layer, 0, n_f + f)),
                pl.BlockSpec((None, tf, d), lambda i, f: (layer, f, 0)),
                hbm, _vec_spec(d), rows.mod_spec(layer, 5, d)]
    args = [hx, w1, w1, w2, xs, g3, mods]
    out_specs = [hbm]
    out_shape = [jax.ShapeDtypeStruct((n_rows, d), f32)]
    scratch = [pltpu.VMEM((tm, d), f32), pltpu.VMEM((tm, d), f32)]
    if has_next:
        in_specs += [_vec_spec(d), rows.mod_spec(layer + 1, 0, d), rows.mod_spec(layer + 1, 1, d)]
        args += [g_next, mods, mods]
        out_specs.append(hbm)
        out_shape.append(jax.ShapeDtypeStruct((n_rows, d), bf16))
        scratch.append(pltpu.VMEM((tm, d), bf16))
    scratch.append(pltpu.SemaphoreType.DMA((3,)))
    out = pl.pallas_call(
        functools.partial(_ffn_kernel, n_f=n_f, n_tiles=n_tiles, has_next=has_next,
                          row_chunk=_tile(tm, FFN_ROW_CHUNK)),
        grid=(n_tiles, n_f),
        in_specs=in_specs,
        out_specs=out_specs,
        out_shape=out_shape,
        scratch_shapes=scratch,
        compiler_params=_cparams("arbitrary", "arbitrary"),
        name="ffn",
    )(*args)
    return out if has_next else (out[0], None)


def kernel(x, c, ctx, c_ctx, ada_w, ada_b, norm_g, w_in, ret_log_decay, conv_w, sgu_w, sgu_b,
           w_branch, w_out, ffn_w_in, ffn_w_out):
    batch, seq, d = x.shape
    ctx_len = ctx.shape[1]
    depth = ada_w.shape[0]
    d_ff = ffn_w_out.shape[1]
    assert batch + 1 <= MOD_ROWS and w_in.shape[2] == N_MIX_COLS + N_BRANCH * d

    rows = _Rows(batch, seq, ctx_len, _tile(batch * ctx_len, 512))
    tm_big = _tile(batch * ctx_len, 1024)
    rows_big = _Rows(batch, seq, ctx_len, tm_big)

    cc = jnp.zeros((MOD_ROWS, d), f32).at[:batch].set(c).at[batch].set(c_ctx)
    mods = _adaln(cc, ada_w, ada_b).reshape(depth * MOD_ROWS * 6, 1, d)

    rope_tabs = _rope_tables(seq)
    dft_x = _dft_tables(seq)
    dft_c = _dft_tables(ctx_len)

    xs = (x.reshape(batch * seq, d), ctx.reshape(batch * ctx_len, d))
    hx = _norm_mod(*xs, norm_g[0, 0:1], mods, rows)

    w_in_b, w_branch_b, w_out_b = w_in.astype(bf16), w_branch.astype(bf16), w_out.astype(bf16)
    ffn_w_in_b, ffn_w_out_b, sgu_w_b = ffn_w_in.astype(bf16), ffn_w_out.astype(bf16), sgu_w.astype(bf16)
    tn = _tile(d, 1024)
    assert N_MIX_COLS % tn == 0

    for l in range(depth):
        sgu_bias = jnp.repeat(jnp.transpose(sgu_b[l]), GROUP_DIM, axis=1)

        last = l + 1 == depth
        n_rows = rows.m_x if last else rows.m

        proj = _matmul(hx, w_in_b, l, N_MIX_COLS, tm_big, N_MIX_COLS // 2, "inproj")
        ya = _retention(proj, ret_log_decay[l], rows, rope_tabs, True)
        yb = _fourier(proj, rows, dft_x, True)
        if last:
            ya_c, yb_c = ya, yb
        else:
            ya_c = _retention(proj, ret_log_decay[l], rows, None, False)
            yb_c = _fourier(proj, rows, dft_c, False)
        yc, yd = _local_mix(proj, sgu_w_b, l, sgu_bias, conv_w, rows, rows.tm, n_rows)
        merged = _merge(hx, (ya, ya_c, yb, yb_c, yc, yd), w_in_b, w_branch_b, l, rows_big, tn, n_rows)
        xs, hx2 = _outproj(merged, w_out_b, xs, norm_g[l, 1:2], norm_g[l, 2:3], mods, l, rows, n_rows)
        g_next = None if last else norm_g[l + 1, 0:1]
        xs, hx = _ffn(hx2, ffn_w_in_b, ffn_w_out_b, xs, norm_g[l, 3:4], g_next, mods, l, rows_big,
                      _tile(d_ff, 512), n_rows)

    return xs.reshape(batch, seq, d)
```

```python
import functools

import jax
import jax.numpy as jnp
from jax import lax
from jax.experimental import pallas as pl
from jax.experimental.pallas import tpu as pltpu

f32 = jnp.float32
bf16 = jnp.bfloat16

GRID_W = 64
MIX_W = 512
N_BRANCH = 4
GROUP_DIM = 128
N_GROUPS = MIX_W // GROUP_DIM
CHUNK = 128
ROPE_BASE = 10000.0
EPS = 1e-6
N_MIX_COLS = 10 * MIX_W
COL_F, COL_U, COL_V, COL_SB, COL_SC, COL_SX = 4, 5, 6, 7, 8, 9
MOD_ROWS = 16
CHUNK_UNROLL = 32
FFN_ROW_CHUNK = 1024
FFN_LAST_ROW_CHUNK = 256
MERGE_ROW_CHUNK = 256
OUTPROJ_ROW_CHUNK = 256

VMEM_LIMIT_BYTES = 56 * 1024 * 1024


def _cparams(*sem):
    return pltpu.CompilerParams(dimension_semantics=sem, vmem_limit_bytes=VMEM_LIMIT_BYTES)


def _tile(n, pref):
    t = min(n, pref)
    while n % t:
        t //= 2
    return t


def _row_chunks(n, chunk, split_tail=False):
    chunk = _tile(n, chunk)
    bounds = list(range(0, n + 1, chunk))
    if split_tail and chunk % 256 == 0:
        bounds.insert(-1, n - chunk // 2)
    return [slice(a, b) for a, b in zip(bounds[:-1], bounds[1:])]


def _rms(x):
    return x * lax.rsqrt(jnp.mean(x * x, axis=-1, keepdims=True) + EPS)


def _adaln_kernel(c_ref, w_ref, b_ref, o_ref):
    s = jax.nn.silu(c_ref[...]).astype(bf16)
    o_ref[...] = jnp.dot(s, w_ref[...].astype(bf16), preferred_element_type=f32) + b_ref[...]


def _adaln(cc, ada_w, ada_b):
    depth, d, n6 = ada_w.shape
    tn = _tile(n6, 1024)
    return pl.pallas_call(
        _adaln_kernel,
        grid=(depth, n6 // tn),
        in_specs=[
            pl.BlockSpec((MOD_ROWS, d), lambda l, j: (0, 0)),
            pl.BlockSpec((None, d, tn), lambda l, j: (l, 0, j)),
            pl.BlockSpec((None, 1, tn), lambda l, j: (l, 0, j)),
        ],
        out_specs=pl.BlockSpec((None, MOD_ROWS, tn), lambda l, j: (l, 0, j)),
        out_shape=jax.ShapeDtypeStruct((depth, MOD_ROWS, n6), f32),
        compiler_params=_cparams("parallel", "parallel"),
        name="adaln",
    )(cc, ada_w, ada_b.reshape(depth, 1, n6))


class _Rows:
    def __init__(self, batch, seq, ctx_len, tm):
        self.batch, self.seq, self.ctx_len, self.tm = batch, seq, ctx_len, tm
        self.m_x, self.m_c = batch * seq, batch * ctx_len
        self.m = self.m_x + self.m_c
        assert seq % tm == 0 and self.m_c % tm == 0
        self.n_x_tiles = self.m_x // tm
        self.n_tiles = self.m // tm
        self.per_batch = seq // tm

    def mod_spec(self, layer, k, d):
        def index(i, *_):
            r = jnp.where(i < self.n_x_tiles, i // self.per_batch, self.batch)
            return ((layer * MOD_ROWS + r) * 6 + k, 0, 0)
        return pl.BlockSpec((None, 1, d), index)


def _vec_spec(d):
    return pl.BlockSpec((1, d), lambda i, *_: (0, 0))


def _norm_mod_kernel(x_ref, c_ref, g_ref, sh_ref, sc_ref, hx_ref, *, n_x_tiles):
    def emit(x):
        hx_ref[...] = (_rms(x) * (g_ref[...] * (1.0 + sc_ref[...])) + sh_ref[...]).astype(bf16)

    @pl.when(pl.program_id(0) < n_x_tiles)
    def _():
        emit(x_ref[...])

    @pl.when(pl.program_id(0) >= n_x_tiles)
    def _():
        emit(c_ref[...])


def _norm_mod(x2d, c2d, g, mods, rows):
    d = x2d.shape[1]
    tm, nxt = rows.tm, rows.n_x_tiles
    return pl.pallas_call(
        functools.partial(_norm_mod_kernel, n_x_tiles=nxt),
        grid=(rows.n_tiles,),
        in_specs=[
            pl.BlockSpec((tm, d), lambda i: (jnp.minimum(i, nxt - 1), 0)),
            pl.BlockSpec((tm, d), lambda i: (jnp.maximum(i - nxt, 0), 0)),
            _vec_spec(d),
            rows.mod_spec(0, 0, d),
            rows.mod_spec(0, 1, d),
        ],
        out_specs=pl.BlockSpec((tm, d), lambda i: (i, 0)),
        out_shape=jax.ShapeDtypeStruct((rows.m, d), bf16),
        compiler_params=_cparams("parallel"),
        name="norm_mod",
    )(x2d, c2d, g, mods, mods)


def _matmul_kernel(a_ref, b_ref, o_ref):
    o_ref[...] = jnp.dot(a_ref[...], b_ref[...], preferred_element_type=f32).astype(o_ref.dtype)


def _matmul(a, b_stack, layer, n, tm, tn, name):
    m, k = a.shape
    w_mode = pl.Buffered(1) if n == tn else pl.Buffered(2)
    return pl.pallas_call(
        _matmul_kernel,
        grid=(m // tm, n // tn),
        in_specs=[pl.BlockSpec((tm, k), lambda i, j: (i, 0)),
                  pl.BlockSpec((None, k, tn), lambda i, j: (layer, 0, j), pipeline_mode=w_mode)],
        out_specs=pl.BlockSpec((tm, tn), lambda i, j: (i, j)),
        out_shape=jax.ShapeDtypeStruct((m, n), bf16),
        compiler_params=_cparams("parallel", "arbitrary"),
        name=name,
    )(a, b_stack)


def _rope_tables(seq):
    t = jnp.arange(seq)
    n = GROUP_DIM // 2
    freqs = ROPE_BASE ** (-jnp.arange(0, n, 2, dtype=f32) / n)

    def tab(pos):
        ang = pos.astype(f32)[:, None] * freqs[None, :]
        c, s = jnp.cos(ang), jnp.sin(ang)
        return jnp.concatenate([c, c], -1), jnp.concatenate([-s, s], -1)

    c1, s1 = tab(t // GRID_W)
    c2, s2 = tab(t % GRID_W)
    dst = jnp.arange(GROUP_DIM)
    src = jnp.where((dst % n) < n // 2, dst + n // 2, dst - n // 2)
    perm = (jnp.arange(GROUP_DIM)[:, None] == src[None, :]).astype(bf16)
    return jnp.concatenate([c1, c2], -1), jnp.concatenate([s1, s2], -1), perm


def _retention_kernel(*refs, n_chunks, rope, ctx_len):
    refs = list(refs)
    lg_ref, q_ref, k_ref, v_ref, g_ref = refs[:5]
    pos = 5
    if ctx_len:
        kc_ref, vc_ref = refs[pos:pos + 2]
        pos += 2
    if rope:
        cos_ref, sin_ref, perm_ref = refs[pos:pos + 3]
        pos += 3
    o_ref, qs, ks, sf, sb = refs[pos:]

    c_len, dh = CHUNK, GROUP_DIM
    h = pl.program_id(1)
    lgf, lgb = lg_ref[0, h], lg_ref[1, h]
    scale = dh ** -0.5
    ri = lax.broadcasted_iota(jnp.int32, (c_len, dh), 0).astype(f32)
    ci = lax.broadcasted_iota(jnp.int32, (c_len, dh), 1).astype(f32)
    rel = ri - ci
    decay = (jnp.where(rel >= 0, jnp.exp(lgf * jnp.maximum(rel, 0.0)), 0.0)
             + jnp.where(rel <= 0, jnp.exp(lgb * jnp.maximum(-rel, 0.0)), 0.0))
    kw_f = jnp.exp(lgf * (c_len - 1 - ri))
    kw_b = jnp.exp(lgb * ri)
    qw_f = jnp.exp(lgf * (ri + 1.0))
    qw_b = jnp.exp(lgb * (c_len - ri))
    cd_f = jnp.exp(jnp.full((dh, dh), lgf * c_len, f32))
    cd_b = jnp.exp(jnp.full((dh, dh), lgb * c_len, f32))

    def rows_of(c):
        return pl.ds(pl.multiple_of(c * c_len, c_len), c_len)

    def kt_dot(kb, vw):
        return lax.dot_general(kb, vw, (((0,), (0,)), ((), ())), preferred_element_type=f32)

    def prep(c, carry):
        rows = rows_of(c)
        q, k = q_ref[rows, :], k_ref[rows, :]
        qf, kf = q.astype(f32), k.astype(f32)
        if rope:
            cos, sin = cos_ref[rows, :], sin_ref[rows, :]
            qf = qf * cos + jnp.dot(q, perm_ref[...], preferred_element_type=f32) * sin
            kf = kf * cos + jnp.dot(k, perm_ref[...], preferred_element_type=f32) * sin
        kb = (kf * scale).astype(bf16)
        qs[rows, :] = qf.astype(bf16)
        ks[rows, :] = kb
        v = v_ref[rows, :].astype(f32)
        vw = jnp.concatenate([v * kw_f, v * kw_b], axis=1).astype(bf16)
        ds = kt_dot(kb, vw)
        sf[c] = ds[:, :dh]
        sb[c] = ds[:, dh:]
        return carry

    unroll = min(n_chunks, CHUNK_UNROLL)
    lax.fori_loop(0, n_chunks, prep, 0, unroll=unroll)

    if ctx_len:
        mi = lax.broadcasted_iota(jnp.int32, (ctx_len, dh), 0).astype(f32)
        kcb = (kc_ref[...].astype(f32) * scale).astype(bf16)
        vc = vc_ref[...].astype(f32)
        vcw = jnp.concatenate([vc * jnp.exp(lgf * (ctx_len - 1 - mi)), vc * jnp.exp(lgb * mi)],
                              axis=1).astype(bf16)
        s0 = kt_dot(kcb, vcw)
        s0_f, s0_b = s0[:, :dh], s0[:, dh:]
    else:
        s0_f = s0_b = jnp.zeros((dh, dh), f32)

    def scan_f(c, s):
        nxt = cd_f * s + sf[c]
        sf[c] = s
        return nxt

    def scan_b(j, s):
        c = n_chunks - 1 - j
        nxt = cd_b * s + sb[c]
        sb[c] = s
        return nxt

    lax.fori_loop(0, n_chunks, scan_f, s0_f)
    lax.fori_loop(0, n_chunks, scan_b, s0_b)

    def emit(c, carry):
        rows = rows_of(c)
        q, k, v = qs[rows, :], ks[rows, :], v_ref[rows, :]
        sc = lax.dot_general(q, k, (((1,), (1,)), ((), ())), preferred_element_type=f32) * decay
        inner = jnp.dot(sc.astype(bf16), v, preferred_element_type=f32)
        s2 = jnp.concatenate([sf[c], sb[c]], axis=1).astype(bf16)
        cr = jnp.dot(q, s2, preferred_element_type=f32)
        ret = inner + qw_f * cr[:, :dh] + qw_b * cr[:, dh:]
        g = g_ref[rows, :].astype(f32)
        o_ref[rows, :] = (jax.nn.silu(g) * _rms(ret)).astype(bf16)
        return carry

    lax.fori_loop(0, n_chunks, emit, 0, unroll=unroll)


def _retention(proj, log_decay, rows, rope_tabs, latent):
    seq = rows.seq if latent else rows.ctx_len
    n_chunks = seq // CHUNK
    row0 = 0 if latent else rows.m_x // seq
    hq = MIX_W // GROUP_DIM

    def col(off):
        return pl.BlockSpec((seq, GROUP_DIM), lambda b, h: (row0 + b, off * hq + h))

    in_specs = [pl.BlockSpec(memory_space=pltpu.SMEM), col(0), col(1), col(2), col(3)]
    args = [log_decay, proj, proj, proj, proj]
    if latent:
        cb0 = rows.m_x // rows.ctx_len
        for off in (1, 2):
            in_specs.append(pl.BlockSpec((rows.ctx_len, GROUP_DIM),
                                         lambda b, h, off=off: (cb0 + b, off * hq + h)))
            args.append(proj)
        cos, sin, perm = rope_tabs
        in_specs += [pl.BlockSpec((seq, GROUP_DIM), lambda b, h: (0, 0)),
                     pl.BlockSpec((seq, GROUP_DIM), lambda b, h: (0, 0)),
                     pl.BlockSpec((GROUP_DIM, GROUP_DIM), lambda b, h: (0, 0))]
        args += [cos, sin, perm]

    return pl.pallas_call(
        functools.partial(_retention_kernel, n_chunks=n_chunks, rope=latent,
                          ctx_len=rows.ctx_len if latent else 0),
        grid=(rows.batch, hq),
        in_specs=in_specs,
        out_specs=pl.BlockSpec((seq, GROUP_DIM), lambda b, h: (b, h)),
        out_shape=jax.ShapeDtypeStruct((rows.batch * seq, MIX_W), bf16),
        scratch_shapes=[pltpu.VMEM((seq, GROUP_DIM), bf16), pltpu.VMEM((seq, GROUP_DIM), bf16),
                        pltpu.VMEM((n_chunks, GROUP_DIM, GROUP_DIM), f32),
                        pltpu.VMEM((n_chunks, GROUP_DIM, GROUP_DIM), f32)],
        compiler_params=_cparams("parallel", "parallel"),
        name="retention_x" if latent else "retention_ctx",
    )(*args)


FOURIER_TILE = 512
FOURIER_EXTRA = 16


def _dft_tables(seq):
    def phase(k, n):
        ang = (k % n).astype(f32) * (2.0 * jnp.pi / n)
        return jnp.cos(ang), jnp.sin(ang)

    def cs(n_f, n):
        r = 1 << ((n_f.bit_length() - 1) // 2)
        t = jnp.arange(n, dtype=jnp.int32)
        ca, sa = phase((r * jnp.arange(n_f // r, dtype=jnp.int32))[:, None] * t[None, :], n)
        cb, sb = phase(jnp.arange(r, dtype=jnp.int32)[:, None] * t[None, :], n)
        cos = ca[:, None, :] * cb[None] - sa[:, None, :] * sb[None]
        sin = sa[:, None, :] * cb[None] + ca[:, None, :] * sb[None]
        return cos.reshape(n_f, n), sin.reshape(n_f, n)

    half = seq // 2
    tm = _tile(half, FOURIER_TILE)
    n_tiles = half // tm
    cl, sl = cs(half, seq)
    nyq = jnp.where(jnp.arange(seq) % 2 == 0, 1.0, -1.0).astype(f32)
    extra = jnp.zeros((n_tiles, FOURIER_EXTRA, seq), f32).at[:, 0, :].set(nyq)
    pos = jnp.concatenate([cl.reshape(n_tiles, tm, seq), sl.reshape(n_tiles, tm, seq), extra], axis=1)

    cc, sc = cs(GROUP_DIM, GROUP_DIM)
    norm = (seq * GROUP_DIM) ** -0.5
    mix = jnp.concatenate([jnp.concatenate([cc, cc], 1), jnp.concatenate([-sc, sc], 1)], 0) * norm

    s = jnp.arange(tm)
    flip = ((s[:, None] + s[None, :] == tm) & (s[:, None] >= 1)).astype(bf16)
    return pos.astype(bf16), mix.astype(bf16), flip


def _fourier_kernel(z_ref, pos_ref, mix_ref, flip_ref, o_ref, stash, *, seq, tm, n_tiles):
    j = pl.program_id(1)
    m = n_tiles - 1 - j

    def mixed(cz, sz):
        qs, ps = [], []
        for g in range(N_GROUPS):
            cols = slice(g * GROUP_DIM, (g + 1) * GROUP_DIM)
            szg = jnp.zeros_like(cz[:, cols]) if sz is None else sz[:, cols]
            lhs = jnp.concatenate([cz[:, cols], szg], axis=1).astype(bf16)
            out = jnp.dot(lhs, mix_ref[...], preferred_element_type=f32)
            qs.append(out[:, :GROUP_DIM])
            ps.append(out[:, GROUP_DIM:])
        return jnp.concatenate(qs, axis=1), jnp.concatenate(ps, axis=1)

    r = jnp.dot(pos_ref[...], z_ref[...], preferred_element_type=f32)
    q, p = mixed(r[:tm], r[tm:2 * tm])

    @pl.when(j == 0)
    def _():
        stash[...] = mixed(r[2 * tm:], None)[1]

    above = stash[0:1, :]
    o_ref[pl.ds(pl.multiple_of(m * tm, tm), tm), :] = q.astype(bf16)
    mir = jnp.dot(flip_ref[...], p.astype(bf16), preferred_element_type=f32)
    row = lax.broadcasted_iota(jnp.int32, mir.shape, 0)
    mir = jnp.where(row == 0, above, mir)
    o_ref[pl.ds(pl.multiple_of(seq // 2 + j * tm, tm), tm), :] = mir.astype(bf16)
    stash[...] = p[:FOURIER_EXTRA]


def _fourier(proj, rows, tables, latent):
    seq = rows.seq if latent else rows.ctx_len
    pos_tab, mix_tab, flip_tab = tables
    n_tiles, tab_rows, _ = pos_tab.shape
    tm = flip_tab.shape[0]
    row0 = 0 if latent else rows.m_x // seq
    in_specs = [pl.BlockSpec((seq, MIX_W), lambda b, j: (row0 + b, COL_F)),
                pl.BlockSpec((None, tab_rows, seq), lambda b, j: (n_tiles - 1 - j, 0, 0)),
                pl.BlockSpec((2 * GROUP_DIM, 2 * GROUP_DIM), lambda b, j: (0, 0)),
                pl.BlockSpec((tm, tm), lambda b, j: (0, 0))]
    return pl.pallas_call(
        functools.partial(_fourier_kernel, seq=seq, tm=tm, n_tiles=n_tiles),
        grid=(rows.batch, n_tiles),
        in_specs=in_specs,
        out_specs=pl.BlockSpec((seq, MIX_W), lambda b, j: (b, 0)),
        out_shape=jax.ShapeDtypeStruct((rows.batch * seq, MIX_W), bf16),
        scratch_shapes=[pltpu.VMEM((FOURIER_EXTRA, MIX_W), f32)],
        compiler_params=_cparams("parallel", "arbitrary"),
        name="fourier_x" if latent else "fourier_ctx",
    )(proj, pos_tab, mix_tab, flip_tab)


def _sgu_kernel(u_ref, v_ref, w_ref, b_ref, o_ref, *, tm):
    for ch in range(tm // CHUNK):
        rows = slice(ch * CHUNK, (ch + 1) * CHUNK)
        for g in range(N_GROUPS):
            cols = slice(g * GROUP_DIM, (g + 1) * GROUP_DIM)
            vg = jax.nn.gelu(v_ref[rows, cols].astype(f32))
            dv = vg - jnp.mean(vg, axis=-1, keepdims=True)
            vn = dv * lax.rsqrt(jnp.mean(dv * dv, axis=-1, keepdims=True) + EPS)
            s = jnp.dot(w_ref[g], vn.astype(bf16), preferred_element_type=f32) + b_ref[:, cols]
            o_ref[rows, cols] = (jax.nn.gelu(u_ref[rows, cols].astype(f32)) * s).astype(bf16)


def _conv_kernel(b_ref, c_ref, x_ref, w_ref, o_ref, *, tm, n_x_tiles, ctx_len):
    period = jnp.where(pl.program_id(0) < n_x_tiles, GRID_W, ctx_len)
    rb = max(GRID_W, ctx_len)
    rp = lax.broadcasted_iota(jnp.int32, (rb, GROUP_DIM), 0) & (period - 1)
    first, last = rp == 0, rp == period - 1
    for r0 in range(0, tm, rb):
        rows = slice(r0, r0 + rb)
        for c0 in range(0, MIX_W, GROUP_DIM):
            cols = slice(c0, c0 + GROUP_DIM)
            y = c_ref[rows, cols].astype(f32) * x_ref[rows, cols].astype(f32)
            prev = jnp.where(first, 0.0, pltpu.roll(y, 1, axis=0))
            nxt = jnp.where(last, 0.0, pltpu.roll(y, rb - 1, axis=0))
            conv = w_ref[0:1, cols] * prev + w_ref[1:2, cols] * y + w_ref[2:3, cols] * nxt
            o_ref[rows, cols] = (b_ref[rows, cols].astype(f32) * conv).astype(bf16)


def _local_mix_kernel(u_ref, v_ref, ws_ref, bs_ref, b_ref, c_ref, x_ref, wc_ref, oc_ref, od_ref,
                      *, tm, n_x_tiles, ctx_len):
    _sgu_kernel(u_ref, v_ref, ws_ref, bs_ref, oc_ref, tm=tm)
    _conv_kernel(b_ref, c_ref, x_ref, wc_ref, od_ref, tm=tm, n_x_tiles=n_x_tiles, ctx_len=ctx_len)


def _local_mix(proj, sgu_w, layer, sgu_bias, conv_w, rows, tm, n_rows):
    assert GRID_W & (GRID_W - 1) == 0 and rows.ctx_len & (rows.ctx_len - 1) == 0
    assert tm % GRID_W == 0 and tm % rows.ctx_len == 0 and tm % CHUNK == 0

    def col(c):
        return pl.BlockSpec((tm, MIX_W), lambda i: (i, c))

    out = jax.ShapeDtypeStruct((n_rows, MIX_W), bf16)
    return pl.pallas_call(
        functools.partial(_local_mix_kernel, tm=tm, n_x_tiles=rows.m_x // tm, ctx_len=rows.ctx_len),
        grid=(n_rows // tm,),
        in_specs=[col(COL_U), col(COL_V),
                  pl.BlockSpec((None, N_GROUPS, CHUNK, CHUNK), lambda i: (layer, 0, 0, 0)),
                  pl.BlockSpec((CHUNK, MIX_W), lambda i: (0, 0)),
                  col(COL_SB), col(COL_SC), col(COL_SX),
                  pl.BlockSpec((None, 3, MIX_W), lambda i: (layer, 0, 0))],
        out_specs=[col(0), col(0)],
        out_shape=[out, out],
        compiler_params=_cparams("parallel"),
        name="local_mix",
    )(proj, proj, sgu_w, sgu_bias, proj, proj, proj, conv_w)


def _merge_kernel(hx_ref, yax_ref, yac_ref, ybx_ref, ybc_ref, yc_ref, yd_ref, wg0_ref, wg1_ref, wb_ref,
                  o_ref, acc, *, row_chunk, n_x_tiles):
    pair = pl.program_id(2)
    latent = pl.program_id(0) < n_x_tiles
    wb0_ref, wb1_ref = wb_ref.at[0:MIX_W], wb_ref.at[MIX_W:2 * MIX_W]

    def term(rows, y, wg_ref, wb_ref):
        gate = jax.nn.sigmoid(jnp.dot(hx_ref[rows, :], wg_ref[...], preferred_element_type=f32))
        return gate * jnp.dot(y, wb_ref[...], preferred_element_type=f32)

    chunks = _row_chunks(hx_ref.shape[0], row_chunk)

    @pl.when(pair == 0)
    def _():
        for rows in chunks:
            ya = jnp.where(latent, yax_ref[rows, :], yac_ref[rows, :])
            yb = jnp.where(latent, ybx_ref[rows, :], ybc_ref[rows, :])
            acc[rows, :] = term(rows, ya, wg0_ref, wb0_ref) + term(rows, yb, wg1_ref, wb1_ref)

    @pl.when(pair == 1)
    def _():
        for rows in chunks:
            t = acc[rows, :] + term(rows, yc_ref[rows, :], wg0_ref, wb0_ref)
            o_ref[rows, :] = (t + term(rows, yd_ref[rows, :], wg1_ref, wb1_ref)).astype(bf16)


def _merge(hx, ys, w_in, w_branch, layer, rows, tn, n_rows):
    assert N_BRANCH == 4
    d = hx.shape[1]
    m = n_rows
    tm = rows.tm
    nxt = rows.n_x_tiles
    nj = d // tn
    gate0 = N_MIX_COLS // tn
    y_spec = pl.BlockSpec((tm, MIX_W), lambda i, j, p: (i, 0))
    yx_spec = pl.BlockSpec((tm, MIX_W), lambda i, j, p: (jnp.minimum(i, nxt - 1), 0))
    yc_spec = pl.BlockSpec((tm, MIX_W), lambda i, j, p: (jnp.maximum(i - nxt, 0), 0))

    def wg_spec(k):
        return pl.BlockSpec((None, d, tn), lambda i, j, p: (layer, 0, gate0 + (2 * p + k) * nj + j))

    w_pairs = w_branch.reshape(w_branch.shape[0], N_BRANCH // 2, 2 * MIX_W, d)
    wb_spec = pl.BlockSpec((None, None, 2 * MIX_W, tn), lambda i, j, p: (layer, p, 0, j))

    return pl.pallas_call(
        functools.partial(_merge_kernel, row_chunk=_tile(tm, MERGE_ROW_CHUNK), n_x_tiles=nxt),
        grid=(m // tm, nj, N_BRANCH // 2),
        in_specs=[pl.BlockSpec((tm, d), lambda i, j, p: (i, 0)), yx_spec, yc_spec, yx_spec, yc_spec, y_spec, y_spec,
                  wg_spec(0), wg_spec(1), wb_spec],
        out_specs=pl.BlockSpec((tm, tn), lambda i, j, p: (i, j)),
        out_shape=jax.ShapeDtypeStruct((m, d), bf16),
        scratch_shapes=[pltpu.VMEM((tm, tn), f32)],
        compiler_params=_cparams("parallel", "arbitrary", "arbitrary"),
        name="merge",
    )(hx, *ys, w_in, w_in, w_pairs)


def _outproj_kernel(m_ref, w_ref, xl_ref, xc_ref, g1_ref, g2_ref, gate_ref, sh_ref, sc_ref, xo_ref, ho_ref,
                    *, row_chunk, n_x_tiles):
    latent = pl.program_id(0) < n_x_tiles
    res_scale = gate_ref[...] * g1_ref[...]
    mod_scale = g2_ref[...] * (1.0 + sc_ref[...])
    for rows in _row_chunks(m_ref.shape[0], row_chunk, split_tail=True):
        mix = jnp.dot(m_ref[rows, :], w_ref[...], preferred_element_type=f32)
        x = xl_ref[rows, :] if xc_ref is None else jnp.where(latent, xl_ref[rows, :], xc_ref[rows, :])
        xn = x + _rms(mix) * res_scale
        xo_ref[rows, :] = xn
        ho_ref[rows, :] = (_rms(xn) * mod_scale + sh_ref[...]).astype(bf16)


def _outproj(merged, w_out, xs, g1, g2, mods, layer, rows, n_rows):
    d = merged.shape[1]
    tm, nxt = rows.tm, rows.n_x_tiles
    row = pl.BlockSpec((tm, d), lambda i: (i, 0))
    if isinstance(xs, tuple):
        x_specs = [pl.BlockSpec((tm, d), lambda i: (jnp.minimum(i, nxt - 1), 0)),
                   pl.BlockSpec((tm, d), lambda i: (jnp.maximum(i - nxt, 0), 0))]
        kernel = functools.partial(_outproj_kernel, row_chunk=OUTPROJ_ROW_CHUNK, n_x_tiles=nxt)
    else:
        xs, x_specs = (xs,), [row]

        def kernel(m_ref, w_ref, x_ref, *rest):
            _outproj_kernel(m_ref, w_ref, x_ref, None, *rest, row_chunk=OUTPROJ_ROW_CHUNK, n_x_tiles=nxt)

    return pl.pallas_call(
        kernel,
        grid=(n_rows // tm,),
        in_specs=[row, pl.BlockSpec((None, d, d), lambda i: (layer, 0, 0), pipeline_mode=pl.Buffered(1)),
                  *x_specs, _vec_spec(d), _vec_spec(d),
                  rows.mod_spec(layer, 2, d), rows.mod_spec(layer, 3, d), rows.mod_spec(layer, 4, d)],
        out_specs=[row, row],
        out_shape=[jax.ShapeDtypeStruct((n_rows, d), f32), jax.ShapeDtypeStruct((n_rows, d), bf16)],
        compiler_params=_cparams("parallel"),
        name="outproj",
    )(merged, w_out, *xs, g1, g2, mods, mods, mods)


def _ffn_kernel(*refs, n_f, n_tiles, has_next, row_chunk):
    if has_next:
        (h_ref, wa_ref, wb_ref, w2_ref, x_hbm, g3_ref, gate_ref, gn_ref, sh_ref, sc_ref,
         xo_hbm, ho_hbm, acc, xbuf, hbuf, sems) = refs
    else:
        h_ref, wa_ref, wb_ref, w2_ref, x_hbm, g3_ref, gate_ref, xo_hbm, acc, xbuf, sems = refs
    i, f = pl.program_id(0), pl.program_id(1)
    tm = h_ref.shape[0]

    def tile_rows(t):
        return pl.ds(pl.multiple_of(t * tm, tm), tm)

    def x_in(t):
        return pltpu.make_async_copy(x_hbm.at[tile_rows(t), :], xbuf, sems.at[0])

    def outs(t):
        copies = [pltpu.make_async_copy(xbuf, xo_hbm.at[tile_rows(t), :], sems.at[1])]
        if has_next:
            copies.append(pltpu.make_async_copy(hbuf, ho_hbm.at[tile_rows(t), :], sems.at[2]))
        return copies

    @pl.when(f == 1)
    def _():
        @pl.when(i > 0)
        def _():
            for cp in outs(i - 1):
                cp.wait()
        x_in(i).start()

    def step(first, last):
        if last:
            x_in(i).wait()
            res_scale = gate_ref[...] * g3_ref[...]
            if has_next:
                mod_scale = gn_ref[...] * (1.0 + sc_ref[...])
        for rows in _row_chunks(tm, FFN_LAST_ROW_CHUNK if last else row_chunk, split_tail=last):
            h = h_ref[rows, :]
            a = jnp.dot(h, wa_ref[...], preferred_element_type=f32)
            b = jnp.dot(h, wb_ref[...], preferred_element_type=f32)
            p = jnp.dot((jax.nn.silu(a) * b).astype(bf16), w2_ref[...], preferred_element_type=f32)
            if not first:
                p = acc[rows, :] + p
            if not last:
                acc[rows, :] = p
            else:
                xn = xbuf[rows, :] + _rms(p) * res_scale
                xbuf[rows, :] = xn
                if has_next:
                    hbuf[rows, :] = (_rms(xn) * mod_scale + sh_ref[...]).astype(bf16)
        if last:
            for cp in outs(i):
                cp.start()

            @pl.when(i == n_tiles - 1)
            def _():
                for cp in outs(i):
                    cp.wait()

    pl.when(f == 0)(lambda: step(True, False))
    pl.when(f == n_f - 1)(lambda: step(False, True))
    pl.when((f > 0) & (f < n_f - 1))(lambda: step(False, False))


def _ffn(hx, w1, w2, xs, g3, g_next, mods, layer, rows, tf, n_rows):
    d = xs.shape[1]
    n_f = w2.shape[1] // tf
    assert n_f >= 3
    tm = rows.tm
    n_tiles = n_rows // tm
    has_next = g_next is not None
    row = pl.BlockSpec((tm, d), lambda i, f: (i, 0))
    hbm = pl.BlockSpec(memory_space=pl.ANY)
    in_specs = [row,
                pl.BlockSpec((None, d, tf), lambda i, f: (layer, 0, f)),
                pl.BlockSpec((None, d, tf), lambda i, f: (layer, 0, n_f + f)),
                pl.BlockSpec((None, tf, d), lambda i, f: (layer, f, 0)),
                hbm, _vec_spec(d), rows.mod_spec(layer, 5, d)]
    args = [hx, w1, w1, w2, xs, g3, mods]
    out_specs = [hbm]
    out_shape = [jax.ShapeDtypeStruct((n_rows, d), f32)]
    scratch = [pltpu.VMEM((tm, d), f32), pltpu.VMEM((tm, d), f32)]
    if has_next:
        in_specs += [_vec_spec(d), rows.mod_spec(layer + 1, 0, d), rows.mod_spec(layer + 1, 1, d)]
        args += [g_next, mods, mods]
        out_specs.append(hbm)
        out_shape.append(jax.ShapeDtypeStruct((n_rows, d), bf16))
        scratch.append(pltpu.VMEM((tm, d), bf16))
    scratch.append(pltpu.SemaphoreType.DMA((3,)))
    out = pl.pallas_call(
        functools.partial(_ffn_kernel, n_f=n_f, n_tiles=n_tiles, has_next=has_next,
                          row_chunk=_tile(tm, FFN_ROW_CHUNK)),
        grid=(n_tiles, n_f),
        in_specs=in_specs,
        out_specs=out_specs,
        out_shape=out_shape,
        scratch_shapes=scratch,
        compiler_params=_cparams("arbitrary", "arbitrary"),
        name="ffn",
    )(*args)
    return out if has_next else (out[0], None)


def kernel(x, c, ctx, c_ctx, ada_w, ada_b, norm_g, w_in, ret_log_decay, conv_w, sgu_w, sgu_b,
           w_branch, w_out, ffn_w_in, ffn_w_out):
    batch, seq, d = x.shape
    ctx_len = ctx.shape[1]
    depth = ada_w.shape[0]
    d_ff = ffn_w_out.shape[1]
    assert batch + 1 <= MOD_ROWS and w_in.shape[2] == N_MIX_COLS + N_BRANCH * d

    rows = _Rows(batch, seq, ctx_len, _tile(batch * ctx_len, 512))
    tm_big = _tile(batch * ctx_len, 1024)
    rows_big = _Rows(batch, seq, ctx_len, tm_big)

    cc = jnp.zeros((MOD_ROWS, d), f32).at[:batch].set(c).at[batch].set(c_ctx)
    mods = _adaln(cc, ada_w, ada_b).reshape(depth * MOD_ROWS * 6, 1, d)

    rope_tabs = _rope_tables(seq)
    dft_x = _dft_tables(seq)
    dft_c = _dft_tables(ctx_len)

    xs = (x.reshape(batch * seq, d), ctx.reshape(batch * ctx_len, d))
    hx = _norm_mod(*xs, norm_g[0, 0:1], mods, rows)

    w_in_b, w_branch_b, w_out_b = w_in.astype(bf16), w_branch.astype(bf16), w_out.astype(bf16)
    ffn_w_in_b, ffn_w_out_b, sgu_w_b = ffn_w_in.astype(bf16), ffn_w_out.astype(bf16), sgu_w.astype(bf16)
    tn = _tile(d, 1024)
    assert N_MIX_COLS % tn == 0

    for l in range(depth):
        sgu_bias = jnp.repeat(jnp.transpose(sgu_b[l]), GROUP_DIM, axis=1)

        last = l + 1 == depth
        n_rows = rows.m_x if last else rows.m

        proj = _matmul(hx, w_in_b, l, N_MIX_COLS, tm_big, N_MIX_COLS, "inproj")
        ya = _retention(proj, ret_log_decay[l], rows, rope_tabs, True)
        yb = _fourier(proj, rows, dft_x, True)
        if last:
            ya_c, yb_c = ya, yb
        else:
            ya_c = _retention(proj, ret_log_decay[l], rows, None, False)
            yb_c = _fourier(proj, rows, dft_c, False)
        yc, yd = _local_mix(proj, sgu_w_b, l, sgu_bias, conv_w, rows, rows.tm, n_rows)
        merged = _merge(hx, (ya, ya_c, yb, yb_c, yc, yd), w_in_b, w_branch_b, l, rows_big, tn, n_rows)
        xs, hx2 = _outproj(merged, w_out_b, xs, norm_g[l, 1:2], norm_g[l, 2:3], mods, l, rows, n_rows)
        g_next = None if last else norm_g[l + 1, 0:1]
        xs, hx = _ffn(hx2, ffn_w_in_b, ffn_w_out_b, xs, norm_g[l, 3:4], g_next, mods, l, rows_big,
                      _tile(d_ff, 512), n_rows)

    return xs.reshape(batch, seq, d)
```

```python
import functools

import jax
import jax.numpy as jnp
from jax import lax
from jax.experimental import pallas as pl
from jax.experimental.pallas import tpu as pltpu

f32 = jnp.float32
bf16 = jnp.bfloat16

GRID_W = 64
MIX_W = 512
N_BRANCH = 4
GROUP_DIM = 128
N_GROUPS = MIX_W // GROUP_DIM
CHUNK = 128
ROPE_BASE = 10000.0
EPS = 1e-6
N_MIX_COLS = 10 * MIX_W
COL_F, COL_U, COL_V, COL_SB, COL_SC, COL_SX = 4, 5, 6, 7, 8, 9
MOD_ROWS = 16

MXU_DIM = 256
ROW_TILE = 512
BIG_ROW_TILE = 1024
COL_TILE = 1024
FFN_COL_TILE = 512
CHUNK_UNROLL = 32
FFN_ROW_CHUNK = 1024
FFN_LAST_ROW_CHUNK = 256
MERGE_ROW_CHUNK = 256
OUTPROJ_ROW_CHUNK = 256
VMEM_LIMIT_BYTES = 56 * 1024 * 1024


def _cparams(*sem):
    return pltpu.CompilerParams(dimension_semantics=sem, vmem_limit_bytes=VMEM_LIMIT_BYTES)


def _tile(n, pref):
    t = min(n, pref)
    while n % t:
        t //= 2
    return t


def _row_chunks(n, chunk, split_tail=False):
    chunk = _tile(n, chunk)
    bounds = list(range(0, n + 1, chunk))
    if split_tail and chunk % MXU_DIM == 0:
        bounds.insert(-1, n - chunk // 2)
    return [slice(a, b) for a, b in zip(bounds[:-1], bounds[1:])]


def _rms(x):
    return x * lax.rsqrt(jnp.mean(x * x, axis=-1, keepdims=True) + EPS)


def _adaln_kernel(c_ref, w_ref, b_ref, o_ref):
    s = jax.nn.silu(c_ref[...]).astype(bf16)
    o_ref[...] = jnp.dot(s, w_ref[...].astype(bf16), preferred_element_type=f32) + b_ref[...]


def _adaln(cc, ada_w, ada_b):
    depth, d, n6 = ada_w.shape
    tn = _tile(n6, COL_TILE)
    return pl.pallas_call(
        _adaln_kernel,
        grid=(depth, n6 // tn),
        in_specs=[
            pl.BlockSpec((MOD_ROWS, d), lambda l, j: (0, 0)),
            pl.BlockSpec((None, d, tn), lambda l, j: (l, 0, j)),
            pl.BlockSpec((None, 1, tn), lambda l, j: (l, 0, j)),
        ],
        out_specs=pl.BlockSpec((None, MOD_ROWS, tn), lambda l, j: (l, 0, j)),
        out_shape=jax.ShapeDtypeStruct((depth, MOD_ROWS, n6), f32),
        compiler_params=_cparams("parallel", "parallel"),
        name="adaln",
    )(cc, ada_w, ada_b.reshape(depth, 1, n6))


class _Rows:
    def __init__(self, batch, seq, ctx_len, tm):
        self.batch, self.seq, self.ctx_len, self.tm = batch, seq, ctx_len, tm
        self.m_x, self.m_c = batch * seq, batch * ctx_len
        self.m = self.m_x + self.m_c
        assert seq % tm == 0 and self.m_c % tm == 0
        self.n_x_tiles = self.m_x // tm
        self.n_tiles = self.m // tm
        self.per_batch = seq // tm

    def mod_spec(self, layer, k, d):
        def index(i, *_):
            r = jnp.where(i < self.n_x_tiles, i // self.per_batch, self.batch)
            return ((layer * MOD_ROWS + r) * 6 + k, 0, 0)
        return pl.BlockSpec((None, 1, d), index)


def _vec_spec(d):
    return pl.BlockSpec((1, d), lambda i, *_: (0, 0))


def _norm_mod_kernel(x_ref, c_ref, g_ref, sh_ref, sc_ref, hx_ref, *, n_x_tiles):
    def emit(x):
        hx_ref[...] = (_rms(x) * (g_ref[...] * (1.0 + sc_ref[...])) + sh_ref[...]).astype(bf16)

    @pl.when(pl.program_id(0) < n_x_tiles)
    def _():
        emit(x_ref[...])

    @pl.when(pl.program_id(0) >= n_x_tiles)
    def _():
        emit(c_ref[...])


def _norm_mod(x2d, c2d, g, mods, rows):
    d = x2d.shape[1]
    tm, nxt = rows.tm, rows.n_x_tiles
    return pl.pallas_call(
        functools.partial(_norm_mod_kernel, n_x_tiles=nxt),
        grid=(rows.n_tiles,),
        in_specs=[
            pl.BlockSpec((tm, d), lambda i: (jnp.minimum(i, nxt - 1), 0)),
            pl.BlockSpec((tm, d), lambda i: (jnp.maximum(i - nxt, 0), 0)),
            _vec_spec(d),
            rows.mod_spec(0, 0, d),
            rows.mod_spec(0, 1, d),
        ],
        out_specs=pl.BlockSpec((tm, d), lambda i: (i, 0)),
        out_shape=jax.ShapeDtypeStruct((rows.m, d), bf16),
        compiler_params=_cparams("parallel"),
        name="norm_mod",
    )(x2d, c2d, g, mods, mods)


def _matmul_kernel(a_ref, b_ref, o_ref):
    o_ref[...] = jnp.dot(a_ref[...], b_ref[...], preferred_element_type=f32).astype(o_ref.dtype)


def _matmul(a, b_stack, layer, n, tm, tn, name):
    m, k = a.shape
    w_mode = pl.Buffered(1) if n == tn else pl.Buffered(2)
    return pl.pallas_call(
        _matmul_kernel,
        grid=(m // tm, n // tn),
        in_specs=[pl.BlockSpec((tm, k), lambda i, j: (i, 0)),
                  pl.BlockSpec((None, k, tn), lambda i, j: (layer, 0, j), pipeline_mode=w_mode)],
        out_specs=pl.BlockSpec((tm, tn), lambda i, j: (i, j)),
        out_shape=jax.ShapeDtypeStruct((m, n), bf16),
        compiler_params=_cparams("parallel", "arbitrary"),
        name=name,
    )(a, b_stack)


def _rope_tables(seq):
    t = jnp.arange(seq)
    n = GROUP_DIM // 2
    freqs = ROPE_BASE ** (-jnp.arange(0, n, 2, dtype=f32) / n)

    def tab(pos):
        ang = pos.astype(f32)[:, None] * freqs[None, :]
        c, s = jnp.cos(ang), jnp.sin(ang)
        return jnp.concatenate([c, c], -1), jnp.concatenate([-s, s], -1)

    c1, s1 = tab(t // GRID_W)
    c2, s2 = tab(t % GRID_W)
    dst = jnp.arange(GROUP_DIM)
    src = jnp.where((dst % n) < n // 2, dst + n // 2, dst - n // 2)
    perm = (jnp.arange(GROUP_DIM)[:, None] == src[None, :]).astype(bf16)
    return jnp.concatenate([c1, c2], -1), jnp.concatenate([s1, s2], -1), perm


def _retention_kernel(*refs, n_chunks, rope, ctx_len):
    refs = list(refs)
    lg_ref, q_ref, k_ref, v_ref, g_ref = refs[:5]
    pos = 5
    if ctx_len:
        kc_ref, vc_ref = refs[pos:pos + 2]
        pos += 2
    if rope:
        cos_ref, sin_ref, perm_ref = refs[pos:pos + 3]
        pos += 3
    o_ref, qs, ks, sf, sb = refs[pos:]

    c_len, dh = CHUNK, GROUP_DIM
    h = pl.program_id(1)
    lgf, lgb = lg_ref[0, h], lg_ref[1, h]
    scale = dh ** -0.5
    ri = lax.broadcasted_iota(jnp.int32, (c_len, dh), 0).astype(f32)
    ci = lax.broadcasted_iota(jnp.int32, (c_len, dh), 1).astype(f32)
    rel = ri - ci
    decay = (jnp.where(rel >= 0, jnp.exp(lgf * jnp.maximum(rel, 0.0)), 0.0)
             + jnp.where(rel <= 0, jnp.exp(lgb * jnp.maximum(-rel, 0.0)), 0.0))
    kw_f = jnp.exp(lgf * (c_len - 1 - ri))
    kw_b = jnp.exp(lgb * ri)
    qw_f = jnp.exp(lgf * (ri + 1.0))
    qw_b = jnp.exp(lgb * (c_len - ri))
    cd_f = jnp.exp(jnp.full((dh, dh), lgf * c_len, f32))
    cd_b = jnp.exp(jnp.full((dh, dh), lgb * c_len, f32))

    def rows_of(c):
        return pl.ds(pl.multiple_of(c * c_len, c_len), c_len)

    def kt_dot(kb, vw):
        return lax.dot_general(kb, vw, (((0,), (0,)), ((), ())), preferred_element_type=f32)

    def prep(c, carry):
        rows = rows_of(c)
        q, k = q_ref[rows, :], k_ref[rows, :]
        qf, kf = q.astype(f32), k.astype(f32)
        if rope:
            cos, sin = cos_ref[rows, :], sin_ref[rows, :]
            qf = qf * cos + jnp.dot(q, perm_ref[...], preferred_element_type=f32) * sin
            kf = kf * cos + jnp.dot(k, perm_ref[...], preferred_element_type=f32) * sin
        kb = (kf * scale).astype(bf16)
        qs[rows, :] = qf.astype(bf16)
        ks[rows, :] = kb
        v = v_ref[rows, :].astype(f32)
        vw = jnp.concatenate([v * kw_f, v * kw_b], axis=1).astype(bf16)
        ds = kt_dot(kb, vw)
        sf[c] = ds[:, :dh]
        sb[c] = ds[:, dh:]
        return carry

    unroll = min(n_chunks, CHUNK_UNROLL)
    lax.fori_loop(0, n_chunks, prep, 0, unroll=unroll)

    if ctx_len:
        mi = lax.broadcasted_iota(jnp.int32, (ctx_len, dh), 0).astype(f32)
        kcb = (kc_ref[...].astype(f32) * scale).astype(bf16)
        vc = vc_ref[...].astype(f32)
        vcw = jnp.concatenate([vc * jnp.exp(lgf * (ctx_len - 1 - mi)), vc * jnp.exp(lgb * mi)],
                              axis=1).astype(bf16)
        s0 = kt_dot(kcb, vcw)
        s0_f, s0_b = s0[:, :dh], s0[:, dh:]
    else:
        s0_f = s0_b = jnp.zeros((dh, dh), f32)

    def scan_f(c, s):
        nxt = cd_f * s + sf[c]
        sf[c] = s
        return nxt

    def scan_b(j, s):
        c = n_chunks - 1 - j
        nxt = cd_b * s + sb[c]
        sb[c] = s
        return nxt

    lax.fori_loop(0, n_chunks, scan_f, s0_f)
    lax.fori_loop(0, n_chunks, scan_b, s0_b)

    def emit(c, carry):
        rows = rows_of(c)
        q, k, v = qs[rows, :], ks[rows, :], v_ref[rows, :]
        sc = lax.dot_general(q, k, (((1,), (1,)), ((), ())), preferred_element_type=f32) * decay
        inner = jnp.dot(sc.astype(bf16), v, preferred_element_type=f32)
        s2 = jnp.concatenate([sf[c], sb[c]], axis=1).astype(bf16)
        cr = jnp.dot(q, s2, preferred_element_type=f32)
        ret = inner + qw_f * cr[:, :dh] + qw_b * cr[:, dh:]
        g = g_ref[rows, :].astype(f32)
        o_ref[rows, :] = (jax.nn.silu(g) * _rms(ret)).astype(bf16)
        return carry

    lax.fori_loop(0, n_chunks, emit, 0, unroll=unroll)


def _retention(proj, log_decay, rows, rope_tabs, latent):
    seq = rows.seq if latent else rows.ctx_len
    n_chunks = seq // CHUNK
    row0 = 0 if latent else rows.m_x // seq
    hq = MIX_W // GROUP_DIM

    def col(off):
        return pl.BlockSpec((seq, GROUP_DIM), lambda b, h: (row0 + b, off * hq + h))

    in_specs = [pl.BlockSpec(memory_space=pltpu.SMEM), col(0), col(1), col(2), col(3)]
    args = [log_decay, proj, proj, proj, proj]
    if latent:
        cb0 = rows.m_x // rows.ctx_len
        for off in (1, 2):
            in_specs.append(pl.BlockSpec((rows.ctx_len, GROUP_DIM),
                                         lambda b, h, off=off: (cb0 + b, off * hq + h)))
            args.append(proj)
        cos, sin, perm = rope_tabs
        once = pl.Buffered(1)
        in_specs += [pl.BlockSpec((seq, GROUP_DIM), lambda b, h: (0, 0), pipeline_mode=once),
                     pl.BlockSpec((seq, GROUP_DIM), lambda b, h: (0, 0), pipeline_mode=once),
                     pl.BlockSpec((GROUP_DIM, GROUP_DIM), lambda b, h: (0, 0), pipeline_mode=once)]
        args += [cos, sin, perm]

    return pl.pallas_call(
        functools.partial(_retention_kernel, n_chunks=n_chunks, rope=latent,
                          ctx_len=rows.ctx_len if latent else 0),
        grid=(rows.batch, hq),
        in_specs=in_specs,
        out_specs=pl.BlockSpec((seq, GROUP_DIM), lambda b, h: (b, h)),
        out_shape=jax.ShapeDtypeStruct((rows.batch * seq, MIX_W), bf16),
        scratch_shapes=[pltpu.VMEM((seq, GROUP_DIM), bf16), pltpu.VMEM((seq, GROUP_DIM), bf16),
                        pltpu.VMEM((n_chunks, GROUP_DIM, GROUP_DIM), f32),
                        pltpu.VMEM((n_chunks, GROUP_DIM, GROUP_DIM), f32)],
        compiler_params=_cparams("parallel", "parallel"),
        name="retention_x" if latent else "retention_ctx",
    )(*args)


FOURIER_TILE = 512
FOURIER_EXTRA = 16


def _dft_tables(seq):
    def phase(k, n):
        ang = (k % n).astype(f32) * (2.0 * jnp.pi / n)
        return jnp.cos(ang), jnp.sin(ang)

    def cs(n_f, n):
        r = 1 << ((n_f.bit_length() - 1) // 2)
        t = jnp.arange(n, dtype=jnp.int32)
        ca, sa = phase((r * jnp.arange(n_f // r, dtype=jnp.int32))[:, None] * t[None, :], n)
        cb, sb = phase(jnp.arange(r, dtype=jnp.int32)[:, None] * t[None, :], n)
        cos = ca[:, None, :] * cb[None] - sa[:, None, :] * sb[None]
        sin = sa[:, None, :] * cb[None] + ca[:, None, :] * sb[None]
        return cos.reshape(n_f, n), sin.reshape(n_f, n)

    half = seq // 2
    tm = _tile(half, FOURIER_TILE)
    n_tiles = half // tm
    cl, sl = cs(half, seq)
    nyq = jnp.where(jnp.arange(seq) % 2 == 0, 1.0, -1.0).astype(f32)
    extra = jnp.zeros((n_tiles, FOURIER_EXTRA, seq), f32).at[:, 0, :].set(nyq)
    pos = jnp.concatenate([cl.reshape(n_tiles, tm, seq), sl.reshape(n_tiles, tm, seq), extra], axis=1)

    cc, sc = cs(GROUP_DIM, GROUP_DIM)
    norm = (seq * GROUP_DIM) ** -0.5
    mix = jnp.concatenate([jnp.concatenate([cc, cc], 1), jnp.concatenate([-sc, sc], 1)], 0) * norm

    s = jnp.arange(tm)
    flip = ((s[:, None] + s[None, :] == tm) & (s[:, None] >= 1)).astype(bf16)
    return pos.astype(bf16), mix.astype(bf16), flip


def _fourier_kernel(z_ref, pos_ref, mix_ref, flip_ref, o_ref, stash, *, seq, tm, n_tiles):
    j = pl.program_id(1)
    m = n_tiles - 1 - j

    def mixed(cz, sz):
        qs, ps = [], []
        for g in range(N_GROUPS):
            cols = slice(g * GROUP_DIM, (g + 1) * GROUP_DIM)
            szg = jnp.zeros_like(cz[:, cols]) if sz is None else sz[:, cols]
            lhs = jnp.concatenate([cz[:, cols], szg], axis=1).astype(bf16)
            out = jnp.dot(lhs, mix_ref[...], preferred_element_type=f32)
            qs.append(out[:, :GROUP_DIM])
            ps.append(out[:, GROUP_DIM:])
        return jnp.concatenate(qs, axis=1), jnp.concatenate(ps, axis=1)

    r = jnp.dot(pos_ref[...], z_ref[...], preferred_element_type=f32)
    q, p = mixed(r[:tm], r[tm:2 * tm])

    @pl.when(j == 0)
    def _():
        stash[...] = mixed(r[2 * tm:], None)[1]

    above = stash[0:1, :]
    o_ref[pl.ds(pl.multiple_of(m * tm, tm), tm), :] = q.astype(bf16)
    mir = jnp.dot(flip_ref[...], p.astype(bf16), preferred_element_type=f32)
    row = lax.broadcasted_iota(jnp.int32, mir.shape, 0)
    mir = jnp.where(row == 0, above, mir)
    o_ref[pl.ds(pl.multiple_of(seq // 2 + j * tm, tm), tm), :] = mir.astype(bf16)
    stash[...] = p[:FOURIER_EXTRA]


def _fourier(proj, rows, tables, latent):
    seq = rows.seq if latent else rows.ctx_len
    pos_tab, mix_tab, flip_tab = tables
    n_tiles, tab_rows, _ = pos_tab.shape
    tm = flip_tab.shape[0]
    row0 = 0 if latent else rows.m_x // seq
    in_specs = [pl.BlockSpec((seq, MIX_W), lambda b, j: (row0 + b, COL_F)),
                pl.BlockSpec((None, tab_rows, seq), lambda b, j: (n_tiles - 1 - j, 0, 0)),
                pl.BlockSpec((2 * GROUP_DIM, 2 * GROUP_DIM), lambda b, j: (0, 0), pipeline_mode=pl.Buffered(1)),
                pl.BlockSpec((tm, tm), lambda b, j: (0, 0), pipeline_mode=pl.Buffered(1))]
    return pl.pallas_call(
        functools.partial(_fourier_kernel, seq=seq, tm=tm, n_tiles=n_tiles),
        grid=(rows.batch, n_tiles),
        in_specs=in_specs,
        out_specs=pl.BlockSpec((seq, MIX_W), lambda b, j: (b, 0)),
        out_shape=jax.ShapeDtypeStruct((rows.batch * seq, MIX_W), bf16),
        scratch_shapes=[pltpu.VMEM((FOURIER_EXTRA, MIX_W), f32)],
        compiler_params=_cparams("parallel", "arbitrary"),
        name="fourier_x" if latent else "fourier_ctx",
    )(proj, pos_tab, mix_tab, flip_tab)


def _sgu_kernel(u_ref, v_ref, w_ref, b_ref, o_ref, *, tm):
    for ch in range(tm // CHUNK):
        rows = slice(ch * CHUNK, (ch + 1) * CHUNK)
        for g in range(N_GROUPS):
            cols = slice(g * GROUP_DIM, (g + 1) * GROUP_DIM)
            vg = jax.nn.gelu(v_ref[rows, cols].astype(f32))
            dv = vg - jnp.mean(vg, axis=-1, keepdims=True)
            vn = dv * lax.rsqrt(jnp.mean(dv * dv, axis=-1, keepdims=True) + EPS)
            s = jnp.dot(w_ref[g], vn.astype(bf16), preferred_element_type=f32) + b_ref[:, cols]
            o_ref[rows, cols] = (jax.nn.gelu(u_ref[rows, cols].astype(f32)) * s).astype(bf16)


def _conv_kernel(b_ref, c_ref, x_ref, w_ref, o_ref, *, tm, n_x_tiles, ctx_len):
    period = jnp.where(pl.program_id(0) < n_x_tiles, GRID_W, ctx_len)
    rb = max(GRID_W, ctx_len)
    rp = lax.broadcasted_iota(jnp.int32, (rb, GROUP_DIM), 0) & (period - 1)
    first, last = rp == 0, rp == period - 1
    for r0 in range(0, tm, rb):
        rows = slice(r0, r0 + rb)
        for c0 in range(0, MIX_W, GROUP_DIM):
            cols = slice(c0, c0 + GROUP_DIM)
            y = c_ref[rows, cols].astype(f32) * x_ref[rows, cols].astype(f32)
            prev = jnp.where(first, 0.0, pltpu.roll(y, 1, axis=0))
            nxt = jnp.where(last, 0.0, pltpu.roll(y, rb - 1, axis=0))
            conv = w_ref[0:1, cols] * prev + w_ref[1:2, cols] * y + w_ref[2:3, cols] * nxt
            o_ref[rows, cols] = (b_ref[rows, cols].astype(f32) * conv).astype(bf16)


def _local_mix_kernel(u_ref, v_ref, ws_ref, bs_ref, b_ref, c_ref, x_ref, wc_ref, oc_ref, od_ref,
                      *, tm, n_x_tiles, ctx_len):
    _sgu_kernel(u_ref, v_ref, ws_ref, bs_ref, oc_ref, tm=tm)
    _conv_kernel(b_ref, c_ref, x_ref, wc_ref, od_ref, tm=tm, n_x_tiles=n_x_tiles, ctx_len=ctx_len)


def _local_mix(proj, sgu_w, layer, sgu_bias, conv_w, rows, tm, n_rows):
    assert GRID_W & (GRID_W - 1) == 0 and rows.ctx_len & (rows.ctx_len - 1) == 0
    assert tm % GRID_W == 0 and tm % rows.ctx_len == 0 and tm % CHUNK == 0

    def col(c):
        return pl.BlockSpec((tm, MIX_W), lambda i: (i, c))

    out = jax.ShapeDtypeStruct((n_rows, MIX_W), bf16)
    return pl.pallas_call(
        functools.partial(_local_mix_kernel, tm=tm, n_x_tiles=rows.m_x // tm, ctx_len=rows.ctx_len),
        grid=(n_rows // tm,),
        in_specs=[col(COL_U), col(COL_V),
                  pl.BlockSpec((None, N_GROUPS, CHUNK, CHUNK), lambda i: (layer, 0, 0, 0)),
                  pl.BlockSpec((CHUNK, MIX_W), lambda i: (0, 0)),
                  col(COL_SB), col(COL_SC), col(COL_SX),
                  pl.BlockSpec((None, 3, MIX_W), lambda i: (layer, 0, 0))],
        out_specs=[col(0), col(0)],
        out_shape=[out, out],
        compiler_params=_cparams("parallel"),
        name="local_mix",
    )(proj, proj, sgu_w, sgu_bias, proj, proj, proj, conv_w)


def _merge_kernel(hx_ref, yax_ref, yac_ref, ybx_ref, ybc_ref, yc_ref, yd_ref, wg0_ref, wg1_ref, wb_ref,
                  o_ref, acc, *, row_chunk, n_x_tiles):
    pair = pl.program_id(2)
    latent = pl.program_id(0) < n_x_tiles
    wb0_ref, wb1_ref = wb_ref.at[0:MIX_W], wb_ref.at[MIX_W:2 * MIX_W]

    def term(rows, y, wg_ref, wb_ref):
        gate = jax.nn.sigmoid(jnp.dot(hx_ref[rows, :], wg_ref[...], preferred_element_type=f32))
        return gate * jnp.dot(y, wb_ref[...], preferred_element_type=f32)

    chunks = _row_chunks(hx_ref.shape[0], row_chunk)

    @pl.when(pair == 0)
    def _():
        for rows in chunks:
            ya = jnp.where(latent, yax_ref[rows, :], yac_ref[rows, :])
            yb = jnp.where(latent, ybx_ref[rows, :], ybc_ref[rows, :])
            acc[rows, :] = term(rows, ya, wg0_ref, wb0_ref) + term(rows, yb, wg1_ref, wb1_ref)

    @pl.when(pair == 1)
    def _():
        for rows in chunks:
            t = acc[rows, :] + term(rows, yc_ref[rows, :], wg0_ref, wb0_ref)
            o_ref[rows, :] = (t + term(rows, yd_ref[rows, :], wg1_ref, wb1_ref)).astype(bf16)


def _merge(hx, ys, w_in, w_branch, layer, rows, tn, n_rows):
    assert N_BRANCH == 4
    d = hx.shape[1]
    m = n_rows
    tm = rows.tm
    nxt = rows.n_x_tiles
    nj = d // tn
    gate0 = N_MIX_COLS // tn
    y_spec = pl.BlockSpec((tm, MIX_W), lambda i, j, p: (i, 0))
    yx_spec = pl.BlockSpec((tm, MIX_W), lambda i, j, p: (jnp.minimum(i, nxt - 1), 0))
    yc_spec = pl.BlockSpec((tm, MIX_W), lambda i, j, p: (jnp.maximum(i - nxt, 0), 0))

    def wg_spec(k):
        return pl.BlockSpec((None, d, tn), lambda i, j, p: (layer, 0, gate0 + (2 * p + k) * nj + j))

    w_pairs = w_branch.reshape(w_branch.shape[0], N_BRANCH // 2, 2 * MIX_W, d)
    wb_spec = pl.BlockSpec((None, None, 2 * MIX_W, tn), lambda i, j, p: (layer, p, 0, j))

    return pl.pallas_call(
        functools.partial(_merge_kernel, row_chunk=_tile(tm, MERGE_ROW_CHUNK), n_x_tiles=nxt),
        grid=(m // tm, nj, N_BRANCH // 2),
        in_specs=[pl.BlockSpec((tm, d), lambda i, j, p: (i, 0)), yx_spec, yc_spec, yx_spec, yc_spec, y_spec, y_spec,
                  wg_spec(0), wg_spec(1), wb_spec],
        out_specs=pl.BlockSpec((tm, tn), lambda i, j, p: (i, j)),
        out_shape=jax.ShapeDtypeStruct((m, d), bf16),
        scratch_shapes=[pltpu.VMEM((tm, tn), f32)],
        compiler_params=_cparams("parallel", "arbitrary", "arbitrary"),
        name="merge",
    )(hx, *ys, w_in, w_in, w_pairs)


def _outproj_kernel(m_ref, w_ref, xl_ref, xc_ref, g1_ref, g2_ref, gate_ref, sh_ref, sc_ref, xo_ref, ho_ref,
                    *, row_chunk, n_x_tiles):
    latent = pl.program_id(0) < n_x_tiles
    res_scale = gate_ref[...] * g1_ref[...]
    mod_scale = g2_ref[...] * (1.0 + sc_ref[...])
    for rows in _row_chunks(m_ref.shape[0], row_chunk, split_tail=True):
        mix = jnp.dot(m_ref[rows, :], w_ref[...], preferred_element_type=f32)
        x = xl_ref[rows, :] if xc_ref is None else jnp.where(latent, xl_ref[rows, :], xc_ref[rows, :])
        xn = x + _rms(mix) * res_scale
        xo_ref[rows, :] = xn
        ho_ref[rows, :] = (_rms(xn) * mod_scale + sh_ref[...]).astype(bf16)


def _outproj(merged, w_out, xs, g1, g2, mods, layer, rows, n_rows):
    d = merged.shape[1]
    tm, nxt = rows.tm, rows.n_x_tiles
    row = pl.BlockSpec((tm, d), lambda i: (i, 0))
    if isinstance(xs, tuple):
        x_specs = [pl.BlockSpec((tm, d), lambda i: (jnp.minimum(i, nxt - 1), 0)),
                   pl.BlockSpec((tm, d), lambda i: (jnp.maximum(i - nxt, 0), 0))]
        kernel = functools.partial(_outproj_kernel, row_chunk=OUTPROJ_ROW_CHUNK, n_x_tiles=nxt)
    else:
        xs, x_specs = (xs,), [row]

        def kernel(m_ref, w_ref, x_ref, *rest):
            _outproj_kernel(m_ref, w_ref, x_ref, None, *rest, row_chunk=OUTPROJ_ROW_CHUNK, n_x_tiles=nxt)

    return pl.pallas_call(
        kernel,
        grid=(n_rows // tm,),
        in_specs=[row, pl.BlockSpec((None, d, d), lambda i: (layer, 0, 0), pipeline_mode=pl.Buffered(1)),
                  *x_specs, _vec_spec(d), _vec_spec(d),
                  rows.mod_spec(layer, 2, d), rows.mod_spec(layer, 3, d), rows.mod_spec(layer, 4, d)],
        out_specs=[row, row],
        out_shape=[jax.ShapeDtypeStruct((n_rows, d), f32), jax.ShapeDtypeStruct((n_rows, d), bf16)],
        compiler_params=_cparams("parallel"),
        name="outproj",
    )(merged, w_out, *xs, g1, g2, mods, mods, mods)


def _ffn_kernel(*refs, n_f, n_tiles, has_next, row_chunk):
    if has_next:
        (h_ref, wa_ref, wb_ref, w2_ref, x_hbm, g3_ref, gate_ref, gn_ref, sh_ref, sc_ref,
         xo_hbm, ho_hbm, acc, xbuf, hbuf, sems) = refs
    else:
        h_ref, wa_ref, wb_ref, w2_ref, x_hbm, g3_ref, gate_ref, xo_hbm, acc, xbuf, sems = refs
    i, f = pl.program_id(0), pl.program_id(1)
    tm = h_ref.shape[0]

    def tile_rows(t):
        return pl.ds(pl.multiple_of(t * tm, tm), tm)

    def x_in(t):
        return pltpu.make_async_copy(x_hbm.at[tile_rows(t), :], xbuf, sems.at[0])

    def outs(t):
        copies = [pltpu.make_async_copy(xbuf, xo_hbm.at[tile_rows(t), :], sems.at[1])]
        if has_next:
            copies.append(pltpu.make_async_copy(hbuf, ho_hbm.at[tile_rows(t), :], sems.at[2]))
        return copies

    @pl.when(f == 1)
    def _():
        @pl.when(i > 0)
        def _():
            for cp in outs(i - 1):
                cp.wait()
        x_in(i).start()

    def step(first, last):
        if last:
            x_in(i).wait()
            res_scale = gate_ref[...] * g3_ref[...]
            if has_next:
                mod_scale = gn_ref[...] * (1.0 + sc_ref[...])
        for rows in _row_chunks(tm, FFN_LAST_ROW_CHUNK if last else row_chunk, split_tail=last):
            h = h_ref[rows, :]
            a = jnp.dot(h, wa_ref[...], preferred_element_type=f32)
            b = jnp.dot(h, wb_ref[...], preferred_element_type=f32)
            p = jnp.dot((jax.nn.silu(a) * b).astype(bf16), w2_ref[...], preferred_element_type=f32)
            if not first:
                p = acc[rows, :] + p
            if not last:
                acc[rows, :] = p
            else:
                xn = xbuf[rows, :] + _rms(p) * res_scale
                xbuf[rows, :] = xn
                if has_next:
                    hbuf[rows, :] = (_rms(xn) * mod_scale + sh_ref[...]).astype(bf16)
        if last:
            for cp in outs(i):
                cp.start()

            @pl.when(i == n_tiles - 1)
            def _():
                for cp in outs(i):
                    cp.wait()

    pl.when(f == 0)(lambda: step(True, False))
    pl.when(f == n_f - 1)(lambda: step(False, True))
    pl.when((f > 0) & (f < n_f - 1))(lambda: step(False, False))


def _ffn(hx, w1, w2, xs, g3, g_next, mods, layer, rows, tf, n_rows):
    d = xs.shape[1]
    n_f = w2.shape[1] // tf
    assert n_f >= 3
    tm = rows.tm
    n_tiles = n_rows // tm
    has_next = g_next is not None
    row = pl.BlockSpec((tm, d), lambda i, f: (i, 0))
    hbm = pl.BlockSpec(memory_space=pl.ANY)
    in_specs = [row,
                pl.BlockSpec((None, d, tf), lambda i, f: (layer, 0, f)),
                pl.BlockSpec((None, d, tf), lambda i, f: (layer, 0, n_f + f)),
                pl.BlockSpec((None, tf, d), lambda i, f: (layer, f, 0)),
                hbm, _vec_spec(d), rows.mod_spec(layer, 5, d)]
    args = [hx, w1, w1, w2, xs, g3, mods]
    out_specs = [hbm]
    out_shape = [jax.ShapeDtypeStruct((n_rows, d), f32)]
    scratch = [pltpu.VMEM((tm, d), f32), pltpu.VMEM((tm, d), f32)]
    if has_next:
        in_specs += [_vec_spec(d), rows.mod_spec(layer + 1, 0, d), rows.mod_spec(layer + 1, 1, d)]
        args += [g_next, mods, mods]
        out_specs.append(hbm)
        out_shape.append(jax.ShapeDtypeStruct((n_rows, d), bf16))
        scratch.append(pltpu.VMEM((tm, d), bf16))
    scratch.append(pltpu.SemaphoreType.DMA((3,)))
    out = pl.pallas_call(
        functools.partial(_ffn_kernel, n_f=n_f, n_tiles=n_tiles, has_next=has_next,
                          row_chunk=_tile(tm, FFN_ROW_CHUNK)),
        grid=(n_tiles, n_f),
        in_specs=in_specs,
        out_specs=out_specs,
        out_shape=out_shape,
        scratch_shapes=scratch,
        compiler_params=_cparams("arbitrary", "arbitrary"),
        name="ffn",
    )(*args)
    return out if has_next else (out[0], None)


def kernel(x, c, ctx, c_ctx, ada_w, ada_b, norm_g, w_in, ret_log_decay, conv_w, sgu_w, sgu_b,
           w_branch, w_out, ffn_w_in, ffn_w_out):
    batch, seq, d = x.shape
    ctx_len = ctx.shape[1]
    depth = ada_w.shape[0]
    d_ff = ffn_w_out.shape[1]
    assert batch + 1 <= MOD_ROWS and w_in.shape[2] == N_MIX_COLS + N_BRANCH * d

    rows = _Rows(batch, seq, ctx_len, _tile(batch * ctx_len, ROW_TILE))
    tm_big = _tile(batch * ctx_len, BIG_ROW_TILE)
    rows_big = _Rows(batch, seq, ctx_len, tm_big)

    cc = jnp.zeros((MOD_ROWS, d), f32).at[:batch].set(c).at[batch].set(c_ctx)
    mods = _adaln(cc, ada_w, ada_b).reshape(depth * MOD_ROWS * 6, 1, d)

    rope_tabs = _rope_tables(seq)
    dft_x = _dft_tables(seq)
    dft_c = _dft_tables(ctx_len)

    xs = (x.reshape(batch * seq, d), ctx.reshape(batch * ctx_len, d))
    hx = _norm_mod(*xs, norm_g[0, 0:1], mods, rows)

    w_in_b, w_branch_b, w_out_b = w_in.astype(bf16), w_branch.astype(bf16), w_out.astype(bf16)
    ffn_w_in_b, ffn_w_out_b, sgu_w_b = ffn_w_in.astype(bf16), ffn_w_out.astype(bf16), sgu_w.astype(bf16)
    tn = _tile(d, COL_TILE)
    assert N_MIX_COLS % tn == 0

    for l in range(depth):
        sgu_bias = jnp.repeat(jnp.transpose(sgu_b[l]), GROUP_DIM, axis=1)

        last = l + 1 == depth
        n_rows = rows.m_x if last else rows.m

        proj = _matmul(hx, w_in_b, l, N_MIX_COLS, tm_big, N_MIX_COLS, "inproj")
        ya = _retention(proj, ret_log_decay[l], rows, rope_tabs, True)
        yb = _fourier(proj, rows, dft_x, True)
        if last:
            ya_c, yb_c = ya, yb
        else:
            ya_c = _retention(proj, ret_log_decay[l], rows, None, False)
            yb_c = _fourier(proj, rows, dft_c, False)
        yc, yd = _local_mix(proj, sgu_w_b, l, sgu_bias, conv_w, rows, rows.tm, n_rows)
        merged = _merge(hx, (ya, ya_c, yb, yb_c, yc, yd), w_in_b, w_branch_b, l, rows_big, tn, n_rows)
        xs, hx2 = _outproj(merged, w_out_b, xs, norm_g[l, 1:2], norm_g[l, 2:3], mods, l, rows, n_rows)
        g_next = None if last else norm_g[l + 1, 0:1]
        xs, hx = _ffn(hx2, ffn_w_in_b, ffn_w_out_b, xs, norm_g[l, 3:4], g_next, mods, l, rows_big,
                      _tile(d_ff, FFN_COL_TILE), n_rows)

    return xs.reshape(batch, seq, d)
```
